```python
import math
import jax
import jax.numpy as jnp
from jax import lax
import numpy as np

D_MODEL = 1024
BATCH = 8
SEQ = 2048
DEPTH = 2
DEC_BATCH = 32
DEC_SEQ = 4
PAST_LEN = 16384
PAGE_SIZE = 128

N_EVEN = (DEPTH + 1) // 2
N_ODD = DEPTH // 2
D_CONV = D_MODEL // 2
CONV_W = 31
GLA_HEADS = 4
D_GLA_V = D_MODEL // 2
GLA_DV = D_GLA_V // GLA_HEADS
GLA_DK = GLA_DV // 2
D_GLA_K = GLA_HEADS * GLA_DK
GLA_RANK = 16
GLA_TAU = 16.0
GLA_CHUNK = 64
OFF_Q = 2 * D_CONV
OFF_K = OFF_Q + D_GLA_K
OFF_V = OFF_K + D_GLA_K
OFF_G = OFF_V + D_GLA_V
OFF_A = OFF_G + D_GLA_V
D_IN_EVEN = OFF_A + GLA_RANK
ATT_HEADS = 16
HEAD_DIM = D_MODEL // ATT_HEADS
MOBA_BLOCK = 256
MOBA_TOPK = 3
Q_BLOCK = 128
ROPE_THETA = 10000.0
D_FF = 2816
N_EXPERTS = 8
TOP_K = 2
D_FF_EXPERT = 3584
NORM_EPS = 1e-6

kernel_name = 'hybrid_convgla_moba_moe_step'


def rmsnorm(x, g):
    xf = x.astype(jnp.float32)
    y = xf * lax.rsqrt(jnp.mean(xf * xf, axis=-1, keepdims=True) + NORM_EPS)
    return (y * g.astype(jnp.float32)).astype(x.dtype)


def layernorm(x, g, b):
    xf = x.astype(jnp.float32)
    mu = jnp.mean(xf, axis=-1, keepdims=True)
    var = jnp.mean(jnp.square(xf - mu), axis=-1, keepdims=True)
    y = (xf - mu) * lax.rsqrt(var + NORM_EPS) * g.astype(jnp.float32) + b.astype(jnp.float32)
    return y.astype(x.dtype)


def rope(x, pos):
    half = HEAD_DIM // 2
    inv = jnp.power(ROPE_THETA, -jnp.arange(half, dtype=jnp.float32) / half)
    ang = pos.astype(jnp.float32)[:, None] * inv[None, :]
    cos, sin = jnp.cos(ang)[:, None, :], jnp.sin(ang)[:, None, :]
    xf = x.astype(jnp.float32)
    x1, x2 = xf[..., :half], xf[..., half:]
    return jnp.concatenate([x1 * cos - x2 * sin, x2 * cos + x1 * sin], axis=-1).astype(x.dtype)


def conformer_conv(u, prev, conv_w, conv_b, ln_g, ln_b):
    a, gt = jnp.split(u, 2, axis=-1)
    glu = a * jax.nn.sigmoid(gt)
    full = jnp.concatenate([prev.astype(glu.dtype), glu], axis=1)
    y = lax.conv_general_dilated(full, conv_w[:, None, :].astype(glu.dtype), window_strides=(1,), padding='VALID',
                                 dimension_numbers=('NWC', 'WIO', 'NWC'), feature_group_count=D_CONV) + conv_b
    y = jax.nn.silu(layernorm(y, ln_g, ln_b))
    return y, full[:, -(CONV_W - 1):]


def gla_recurrence(q, k, v, la, s0):
    B, T, H, DK = q.shape
    C = GLA_CHUNK if T % GLA_CHUNK == 0 else T
    nc = T // C

    def to_chunks(t):
        return t.astype(jnp.float32).reshape(B, nc, C, H, t.shape[-1]).transpose(1, 0, 3, 2, 4)

    qc, kc, vc, lc = to_chunks(q), to_chunks(k), to_chunks(v), to_chunks(la)
    causal = jnp.tril(jnp.ones((C, C), dtype=bool))[None, None, :, :, None]

    def step(S, inp):
        qi, ki, vi, li = inp
        b = jnp.cumsum(li, axis=2)
        o_inter = jnp.einsum('bhtd,bhde->bhte', qi * jnp.exp(b), S)
        decay = jnp.exp(jnp.where(causal, b[:, :, :, None, :] - b[:, :, None, :, :], -jnp.inf))
        att = jnp.einsum('bhtsd,bhsd->bhts', qi[:, :, :, None, :] * decay, ki)
        o = o_inter + jnp.einsum('bhts,bhse->bhte', att, vi)
        b_last = b[:, :, -1, :]
        S = jnp.exp(b_last)[..., None] * S + jnp.einsum('bhsd,bhse->bhde', ki * jnp.exp(b_last[:, :, None, :] - b), vi)
        return S, o

    S, o = lax.scan(step, s0.astype(jnp.float32), (qc, kc, vc, lc))
    o = o.transpose(1, 0, 3, 2, 4).reshape(B, T, H, -1)
    return o, S


def conv_gla_mixer(h, conv_prev, gla_prev, w_in, conv_w, conv_b, ln_g, ln_b, w_a2, b_a, gla_g, w_out):
    B, T, _ = h.shape
    proj = h @ w_in
    u = proj[..., :OFF_Q]
    q = proj[..., OFF_Q:OFF_K].reshape(B, T, GLA_HEADS, GLA_DK) * (GLA_DK ** -0.5)
    k = proj[..., OFF_K:OFF_V].reshape(B, T, GLA_HEADS, GLA_DK)
    v = proj[..., OFF_V:OFF_G].reshape(B, T, GLA_HEADS, GLA_DV)
    g = proj[..., OFF_G:OFF_A]
    a = proj[..., OFF_A:]
    la = jax.nn.log_sigmoid((a @ w_a2 + b_a).astype(jnp.float32)).reshape(B, T, GLA_HEADS, GLA_DK) / GLA_TAU
    conv_out, conv_state = conformer_conv(u, conv_prev, conv_w, conv_b, ln_g, ln_b)
    o, gla_state = gla_recurrence(q, k, v, la, gla_prev)
    o = rmsnorm(o, gla_g.reshape(GLA_HEADS, GLA_DV)).astype(h.dtype).reshape(B, T, D_GLA_V) * jax.nn.silu(g)
    y = jnp.concatenate([conv_out.astype(h.dtype), o], axis=-1) @ w_out
    return y, conv_state, gla_state.astype(gla_prev.dtype)


def moba_select(q, kmean, n_valid, n_sel):
    gate = jnp.einsum('...qd,...nd->...qn', q.astype(jnp.float32), kmean.astype(jnp.float32))
    nb = kmean.shape[-2]
    gate = jnp.where(jnp.arange(nb) < n_valid, gate, -jnp.inf)
    _, sel = lax.top_k(gate, n_sel)
    return sel, sel < n_valid


def moba_attend(q, k_own, v_own, own_mask, k_sel, v_sel, sel_valid):
    scale = q.shape[-1] ** -0.5
    s_own = jnp.einsum('...qd,...kd->...qk', q, k_own).astype(jnp.float32) * scale
    s_own = jnp.where(own_mask, s_own, -jnp.inf)
    if k_sel is None:
        p = jax.nn.softmax(s_own, axis=-1).astype(v_own.dtype)
        return jnp.einsum('...qk,...kd->...qd', p, v_own)
    n, bs = k_sel.shape[-3], k_sel.shape[-2]
    s_sel = jnp.einsum('...qd,...qnkd->...qnk', q, k_sel).astype(jnp.float32) * scale
    s_sel = jnp.where(sel_valid[..., None], s_sel, -jnp.inf)
    s = jnp.concatenate([s_sel.reshape(s_sel.shape[:-2] + (n * bs,)), s_own], axis=-1)
    p = jax.nn.softmax(s, axis=-1).astype(v_own.dtype)
    p_sel = p[..., :n * bs].reshape(s_sel.shape)
    p_own = p[..., n * bs:]
    return jnp.einsum('...qnk,...qnkd->...qd', p_sel, v_sel) + jnp.einsum('...qk,...kd->...qd', p_own, v_own)


def moba_prompt(q, k, v):
    B, H, T, HD = q.shape
    nb = -(-T // MOBA_BLOCK)
    padw = ((0, 0), (0, 0), (0, nb * MOBA_BLOCK - T), (0, 0))
    kb_all = jnp.pad(k, padw).reshape(B, H, nb, MOBA_BLOCK, HD)
    vb_all = jnp.pad(v, padw).reshape(B, H, nb, MOBA_BLOCK, HD)
    kmean_all = jnp.mean(kb_all, axis=3, dtype=jnp.float32)
    n_sel = min(MOBA_TOPK, nb - 1)
    nqb = T // Q_BLOCK
    qb = q.reshape(B, H, nqb, Q_BLOCK, HD).transpose(0, 2, 1, 3, 4).reshape(B * nqb, H, Q_BLOCK, HD)
    b_idx = jnp.repeat(jnp.arange(B), nqb)
    j_idx = jnp.tile(jnp.arange(nqb), B)
    heads = jnp.arange(H)[:, None, None]

    def one_block(args):
        qi, b, j = args
        kb, vb = kb_all[b], vb_all[b]
        own = (j * Q_BLOCK) // MOBA_BLOCK
        k_own = lax.dynamic_index_in_dim(kb, own, axis=1, keepdims=False)
        v_own = lax.dynamic_index_in_dim(vb, own, axis=1, keepdims=False)
        qpos = j * Q_BLOCK + jnp.arange(Q_BLOCK)
        kpos = own * MOBA_BLOCK + jnp.arange(MOBA_BLOCK)
        own_mask = kpos[None, :] <= qpos[:, None]
        if n_sel == 0:
            return moba_attend(qi, k_own, v_own, own_mask, None, None, None)
        sel, valid = moba_select(qi, kmean_all[b], own, n_sel)
        return moba_attend(qi, k_own, v_own, own_mask, kb[heads, sel], vb[heads, sel], valid)

    o = lax.map(one_block, (qb, b_idx, j_idx))
    return o.reshape(B, nqb, H, Q_BLOCK, HD).transpose(0, 2, 1, 3, 4).reshape(B, H, T, HD)


def moba_mixer_prompt(h, w_qkv, w_out):
    B, T, _ = h.shape
    qkv = (h @ w_qkv).reshape(B, T, 3, ATT_HEADS, HEAD_DIM)
    pos = jnp.arange(T)
    q = rope(qkv[:, :, 0], pos).transpose(0, 2, 1, 3)
    k = rope(qkv[:, :, 1], pos).transpose(0, 2, 1, 3)
    v = qkv[:, :, 2].transpose(0, 2, 1, 3)
    o = moba_prompt(q, k, v)
    y = o.transpose(0, 2, 1, 3).reshape(B, T, D_MODEL) @ w_out
    np_ = T // PAGE_SIZE
    k_rows = k.reshape(B, ATT_HEADS, np_, PAGE_SIZE, HEAD_DIM).transpose(0, 2, 1, 3, 4)
    v_rows = v.reshape(B, ATT_HEADS, np_, PAGE_SIZE, HEAD_DIM).transpose(0, 2, 1, 3, 4)
    return y, k_rows, v_rows


def moba_mixer_sample(h, k_pool, v_pool, page_table, w_qkv, w_out):
    DB, S, _ = h.shape
    H, HD = ATT_HEADS, HEAD_DIM
    qkv = (h @ w_qkv).reshape(DB, S, 3, H, HD)
    pos = PAST_LEN + jnp.arange(S)
    q = rope(qkv[:, :, 0], pos).transpose(0, 2, 1, 3)
    k = rope(qkv[:, :, 1], pos).transpose(0, 2, 1, 3)
    v = qkv[:, :, 2].transpose(0, 2, 1, 3)
    ppb = MOBA_BLOCK // PAGE_SIZE
    nbp = PAST_LEN // MOBA_BLOCK
    n_own_pages = (PAST_LEN - nbp * MOBA_BLOCK) // PAGE_SIZE
    own_pages = page_table[:, nbp * ppb: nbp * ppb + n_own_pages]
    k_own_past = k_pool[own_pages].transpose(0, 2, 1, 3, 4).reshape(DB, H, n_own_pages * PAGE_SIZE, HD)
    v_own_past = v_pool[own_pages].transpose(0, 2, 1, 3, 4).reshape(DB, H, n_own_pages * PAGE_SIZE, HD)
    k_own = jnp.concatenate([k_own_past.astype(k.dtype), k], axis=2)
    v_own = jnp.concatenate([v_own_past.astype(v.dtype), v], axis=2)
    own_mask = jnp.concatenate([jnp.ones((S, n_own_pages * PAGE_SIZE), dtype=bool),
                                jnp.tril(jnp.ones((S, S), dtype=bool))], axis=1)
    n_sel = min(MOBA_TOPK, nbp)
    if n_sel == 0:
        o = moba_attend(q, k_own, v_own, own_mask, None, None, None)
    else:
        blk_pages = page_table[:, :nbp * ppb]
        rows = k_pool[blk_pages].reshape(DB, nbp, ppb, H, PAGE_SIZE, HD)
        kmean = jnp.mean(rows, axis=(2, 4), dtype=jnp.float32).transpose(0, 2, 1, 3)
        sel, valid = moba_select(q, kmean, nbp, n_sel)
        pages = page_table[jnp.arange(DB)[:, None, None, None, None], sel[..., None] * ppb + jnp.arange(ppb)]
        hh = jnp.arange(H)[None, :, None, None, None]
        k_sel = k_pool[pages, hh].reshape(DB, H, S, n_sel, MOBA_BLOCK, HD).astype(k.dtype)
        v_sel = v_pool[pages, hh].reshape(DB, H, S, n_sel, MOBA_BLOCK, HD).astype(v.dtype)
        o = moba_attend(q, k_own, v_own, own_mask, k_sel, v_sel, valid)
    y = o.transpose(0, 2, 1, 3).reshape(DB, S, D_MODEL) @ w_out
    return y, k, v


def swiglu(x, w_gate, w_up, w_down):
    return (jax.nn.silu(x @ w_gate) * (x @ w_up)) @ w_down


def moe_ffn(x, router_w, w_gate, w_up, w_down):
    shp = x.shape
    xf = x.reshape(-1, shp[-1])
    logits = (xf @ router_w).astype(jnp.float32)
    top_val, top_idx = lax.top_k(logits, TOP_K)
    gates = jax.nn.softmax(top_val, axis=-1)
    dense_gate = jnp.sum(jax.nn.one_hot(top_idx, N_EXPERTS, dtype=jnp.float32) * gates[..., None], axis=-2)
    y = jnp.zeros_like(xf)
    for e in range(N_EXPERTS):
        y = y + dense_gate[:, e:e + 1].astype(xf.dtype) * swiglu(xf, w_gate[e], w_up[e], w_down[e])
    return y.reshape(shp)


def setup_inputs(seed: int = 0) -> dict:
    key = jax.random.key(seed)
    ks = list(jax.random.split(key, 48))
    f32 = jnp.float32

    def nrm(shape, scale):
        return scale * jax.random.normal(ks.pop(), shape, f32)

    def gain(shape):
        return 1.0 + 0.01 * jax.random.normal(ks.pop(), shape, f32)

    n_pages = PAST_LEN // PAGE_SIZE
    n_used = DEC_BATCH * n_pages
    n_pool = n_used + max(1, n_used // 4)
    page_table = jax.random.permutation(ks.pop(), n_pool)[:n_used].reshape(DEC_BATCH, n_pages).astype(jnp.int32)
    D = D_MODEL
    return {
        'x_prompt': nrm((BATCH, SEQ, D), 1.0),
        'x_sample': nrm((DEC_BATCH, DEC_SEQ, D), 1.0),
        'state_conv': nrm((N_EVEN, DEC_BATCH, CONV_W - 1, D_CONV), 0.5),
        'state_gla': nrm((N_EVEN, DEC_BATCH, GLA_HEADS, GLA_DK, GLA_DV), 0.3),
        'cache_k': nrm((N_ODD, n_pool, ATT_HEADS, PAGE_SIZE, HEAD_DIM), 1.0),
        'cache_v': nrm((N_ODD, n_pool, ATT_HEADS, PAGE_SIZE, HEAD_DIM), 1.0),
        'page_table': page_table,
        'norm_mix_even': gain((N_EVEN, D)),
        'w_in_even': nrm((N_EVEN, D, D_IN_EVEN), D ** -0.5),
        'conv_w': nrm((N_EVEN, CONV_W, D_CONV), CONV_W ** -0.5),
        'conv_b': nrm((N_EVEN, D_CONV), 0.01),
        'conv_ln_g': gain((N_EVEN, D_CONV)),
        'conv_ln_b': nrm((N_EVEN, D_CONV), 0.01),
        'gla_w_a2': nrm((N_EVEN, GLA_RANK, D_GLA_K), GLA_RANK ** -0.5),
        'gla_b_a': nrm((N_EVEN, D_GLA_K), 0.01),
        'gla_norm_g': gain((N_EVEN, D_GLA_V)),
        'w_out_even': nrm((N_EVEN, D_CONV + D_GLA_V, D), (D_CONV + D_GLA_V) ** -0.5),
        'norm_ffn_even': gain((N_EVEN, D)),
        'ffn_w_gate': nrm((N_EVEN, D, D_FF), D ** -0.5),
        'ffn_w_up': nrm((N_EVEN, D, D_FF), D ** -0.5),
        'ffn_w_down': nrm((N_EVEN, D_FF, D), D_FF ** -0.5),
        'norm_mix_odd': gain((N_ODD, D)),
        'w_qkv_odd': nrm((N_ODD, D, 3 * D), D ** -0.5),
        'w_out_odd': nrm((N_ODD, D, D), D ** -0.5),
        'norm_ffn_odd': gain((N_ODD, D)),
        'router_w': nrm((N_ODD, D, N_EXPERTS), D ** -0.5),
        'moe_w_gate': nrm((N_ODD, N_EXPERTS, D, D_FF_EXPERT), D ** -0.5),
        'moe_w_up': nrm((N_ODD, N_EXPERTS, D, D_FF_EXPERT), D ** -0.5),
        'moe_w_down': nrm((N_ODD, N_EXPERTS, D_FF_EXPERT, D), D_FF_EXPERT ** -0.5),
        'final_norm_g': gain((D,)),
    }


def reference(x_prompt, x_sample, state_conv, state_gla, cache_k, cache_v, page_table,
              norm_mix_even, w_in_even, conv_w, conv_b, conv_ln_g, conv_ln_b, gla_w_a2, gla_b_a, gla_norm_g,
              w_out_even, norm_ffn_even, ffn_w_gate, ffn_w_up, ffn_w_down,
              norm_mix_odd, w_qkv_odd, w_out_odd, norm_ffn_odd, router_w, moe_w_gate, moe_w_up, moe_w_down,
              final_norm_g):
    B = x_prompt.shape[0]
    hp, hs = x_prompt, x_sample
    conv_p, gla_p, k_p, v_p = [], [], [], []
    conv_s, gla_s, k_s, v_s = [], [], [], []
    for layer in range(DEPTH):
        i = layer // 2
        if layer % 2 == 0:
            mw = (w_in_even[i], conv_w[i], conv_b[i], conv_ln_g[i], conv_ln_b[i], gla_w_a2[i], gla_b_a[i],
                  gla_norm_g[i], w_out_even[i])
            zero_conv = jnp.zeros((B, CONV_W - 1, D_CONV), hp.dtype)
            zero_gla = jnp.zeros((B, GLA_HEADS, GLA_DK, GLA_DV), state_gla.dtype)
            yp, cp, sp = conv_gla_mixer(rmsnorm(hp, norm_mix_even[i]), zero_conv, zero_gla, *mw)
            ys, cs, ss = conv_gla_mixer(rmsnorm(hs, norm_mix_even[i]), state_conv[i], state_gla[i], *mw)
            hp = hp + yp
            hs = hs + ys
            conv_p.append(cp)
            gla_p.append(sp)
            conv_s.append(cs)
            gla_s.append(ss)
            fw = (ffn_w_gate[i], ffn_w_up[i], ffn_w_down[i])
            hp = hp + swiglu(rmsnorm(hp, norm_ffn_even[i]), *fw)
            hs = hs + swiglu(rmsnorm(hs, norm_ffn_even[i]), *fw)
        else:
            yp, kp_rows, vp_rows = moba_mixer_prompt(rmsnorm(hp, norm_mix_odd[i]), w_qkv_odd[i], w_out_odd[i])
            ys, ks_rows, vs_rows = moba_mixer_sample(rmsnorm(hs, norm_mix_odd[i]), cache_k[i], cache_v[i],
                                                     page_table, w_qkv_odd[i], w_out_odd[i])
            hp = hp + yp
            hs = hs + ys
            k_p.append(kp_rows)
            v_p.append(vp_rows)
            k_s.append(ks_rows)
            v_s.append(vs_rows)
            ew = (router_w[i], moe_w_gate[i], moe_w_up[i], moe_w_down[i])
            hp = hp + moe_ffn(rmsnorm(hp, norm_ffn_odd[i]), *ew)
            hs = hs + moe_ffn(rmsnorm(hs, norm_ffn_odd[i]), *ew)
    y_prompt = rmsnorm(hp, final_norm_g)
    y_sample = rmsnorm(hs, final_norm_g)
    new_conv_prompt = jnp.stack(conv_p)
    new_gla_prompt = jnp.stack(gla_p)
    new_k_prompt = jnp.stack(k_p)
    new_v_prompt = jnp.stack(v_p)
    new_conv_sample = jnp.stack(conv_s)
    new_gla_sample = jnp.stack(gla_s)
    new_k_sample = jnp.stack(k_s)
    new_v_sample = jnp.stack(v_s)
    return (y_prompt, y_sample, new_conv_prompt, new_gla_prompt, new_k_prompt, new_v_prompt,
            new_conv_sample, new_gla_sample, new_k_sample, new_v_sample)
```

```python
import functools

import jax
import jax.numpy as jnp
import numpy as np
from jax import lax
from jax.experimental import pallas as pl
from jax.experimental.pallas import tpu as pltpu

F32 = jnp.float32
BF16 = jnp.bfloat16
HIGHEST = lax.Precision.HIGHEST

LANES = 128
NORM_EPS = 1e-6
D_MODEL = 1024
D_CONV = 512
CONV_W = 31
CONV_HALO = 32
GLA_HEADS = 4
GLA_DK = 64
GLA_DV = 128
GLA_TAU = 16.0
GLA_CHUNK = 64
ATT_HEADS = 16
HEAD_DIM = 64
MOBA_BLOCK = 256
MOBA_TOPK = 3
PAGE_SIZE = 128
ROPE_THETA = 10000.0
N_EXPERTS = 8
VMEM_LIMIT = 56 * 1024 * 1024


def _cparams(n_axes):
    return pltpu.CompilerParams(dimension_semantics=("arbitrary",) * n_axes,
                                vmem_limit_bytes=VMEM_LIMIT)


def _rms(x, g):
    return x * lax.rsqrt(jnp.mean(x * x, axis=-1, keepdims=True) + NORM_EPS) * g


def _silu(x):
    return x * jax.nn.sigmoid(x)


def _bdot(a, b):
    return jnp.dot(a.astype(BF16), b.astype(BF16), preferred_element_type=F32)


def _bdot_t(a, b):
    return lax.dot_general(a.astype(BF16), b.astype(BF16), (((1,), (1,)), ((), ())),
                           preferred_element_type=F32)


def _tdot(a, b):
    return lax.dot_general(a.astype(BF16), b.astype(BF16), (((0,), (0,)), ((), ())),
                           preferred_element_type=F32)


def _inproj_kernel(x_ref, g_ref, w_ref, wa1_ref, wa2_ref, ba_ref,
                   glu_ref, q_ref, k_ref, v_ref, gate_ref, la_ref, wbf_ref, wa1bf_ref):
    @pl.when(pl.program_id(0) == 0)
    def _():
        wbf_ref[...] = w_ref[...].astype(BF16)
        wa1bf_ref[...] = wa1_ref[...].astype(BF16)

    xn = _rms(x_ref[...], g_ref[...]).astype(BF16)
    y = jnp.dot(xn, wbf_ref[...], preferred_element_type=F32)
    glu_ref[...] = y[:, :D_CONV] * jax.nn.sigmoid(y[:, D_CONV:2 * D_CONV])
    q_ref[...] = y[:, 1024:1280] * (GLA_DK ** -0.5)
    k_ref[...] = y[:, 1280:1536]
    v_ref[...] = y[:, 1536:2048]
    gate_ref[...] = y[:, 2048:2560]
    a = jnp.dot(xn, wa1bf_ref[...], preferred_element_type=F32)
    z = jnp.dot(a, wa2_ref[...], precision=HIGHEST, preferred_element_type=F32) + ba_ref[...]
    log_sig = jnp.minimum(z, 0.0) - jnp.log(1.0 + jnp.exp(-jnp.abs(z)))
    la_ref[...] = log_sig * (1.0 / GLA_TAU)


def _inproj(x, g, w_in, w_a2, b_a, tm):
    m = x.shape[0]
    n_main = 2560
    wa1 = jnp.pad(w_in[:, n_main:], ((0, 0), (0, LANES - (w_in.shape[1] - n_main))))
    wa2 = jnp.pad(w_a2, ((0, LANES - w_a2.shape[0]), (0, 0)))
    row = lambda i: (i, 0)
    fixed = lambda i: (0, 0)
    outs = [(D_CONV, F32), (256, F32), (256, F32), (512, F32), (512, F32), (256, F32)]
    return pl.pallas_call(
        _inproj_kernel,
        grid=(m // tm,),
        in_specs=[pl.BlockSpec((tm, D_MODEL), row),
                  pl.BlockSpec((1, D_MODEL), fixed),
                  pl.BlockSpec((D_MODEL, n_main), fixed),
                  pl.BlockSpec((D_MODEL, LANES), fixed),
                  pl.BlockSpec((LANES, 256), fixed),
                  pl.BlockSpec((1, 256), fixed)],
        out_specs=[pl.BlockSpec((tm, n), row) for n, _ in outs],
        out_shape=[jax.ShapeDtypeStruct((m, n), dt) for n, dt in outs],
        scratch_shapes=[pltpu.VMEM((D_MODEL, n_main), BF16), pltpu.VMEM((D_MODEL, LANES), BF16)],
        compiler_params=_cparams(1),
        name="inproj",
    )(x, g.reshape(1, -1), w_in, wa1, wa2, b_a.reshape(1, -1))


def _conv_kernel(glu_ref, prev_ref, w_ref, b_ref, lg_ref, lb_ref, out_ref, st_ref, full_ref, *, tt, rc):
    t = pl.program_id(1)
    lo = CONV_HALO - (CONV_W - 1)

    @pl.when(t == 0)
    def _():
        full_ref[0:lo, :] = jnp.zeros((lo, D_CONV), F32)
        full_ref[lo:CONV_HALO, :] = prev_ref[...]

    full_ref[CONV_HALO:CONV_HALO + tt, :] = glu_ref[...]
    for c in range(tt // rc):
        base = c * rc + lo
        acc = full_ref[base:base + rc, :] * w_ref[0:1, :]
        for j in range(1, CONV_W):
            acc = acc + full_ref[base + j:base + j + rc, :] * w_ref[j:j + 1, :]
        y = acc + b_ref[...]
        mu = jnp.mean(y, axis=-1, keepdims=True)
        d = y - mu
        var = jnp.mean(d * d, axis=-1, keepdims=True)
        yn = d * lax.rsqrt(var + NORM_EPS) * lg_ref[...] + lb_ref[...]
        out_ref[c * rc:(c + 1) * rc, :] = _silu(yn).astype(out_ref.dtype)

    tail = full_ref[tt + lo:tt + CONV_HALO, :]

    @pl.when(t == pl.num_programs(1) - 1)
    def _():
        st_ref[...] = tail

    full_ref[lo:CONV_HALO, :] = tail


def _conv_module(glu, prev, conv_w, conv_b, ln_g, ln_b, tt):
    b, t, _ = glu.shape
    rc = min(tt, 32)
    fixed = lambda i, j: (0, 0)
    return pl.pallas_call(
        functools.partial(_conv_kernel, tt=tt, rc=rc),
        grid=(b, t // tt),
        in_specs=[pl.BlockSpec((None, tt, D_CONV), lambda i, j: (i, j, 0)),
                  pl.BlockSpec((None, CONV_W - 1, D_CONV), lambda i, j: (i, 0, 0)),
                  pl.BlockSpec((CONV_W, D_CONV), fixed),
                  pl.BlockSpec((1, D_CONV), fixed),
                  pl.BlockSpec((1, D_CONV), fixed),
                  pl.BlockSpec((1, D_CONV), fixed)],
        out_specs=[pl.BlockSpec((None, tt, D_CONV), lambda i, j: (i, j, 0)),
                   pl.BlockSpec((None, CONV_W - 1, D_CONV), lambda i, j: (i, 0, 0))],
        out_shape=[jax.ShapeDtypeStruct((b, t, D_CONV), BF16),
                   jax.ShapeDtypeStruct((b, CONV_W - 1, D_CONV), F32)],
        scratch_shapes=[pltpu.VMEM((CONV_HALO + tt, D_CONV), F32)],
        compiler_params=_cparams(2),
        name="conv_module",
    )(glu, prev, conv_w, conv_b.reshape(1, -1), ln_g.reshape(1, -1), ln_b.reshape(1, -1))


def _gla_levels(c):
    levels, g_big = [], c
    while g_big > 1:
        g_small = max(g_big // 4, 1)
        levels.append((g_big, g_small))
        g_big = g_small
    return levels


def _gla_decay_matrices(c):
    t = np.arange(c)[:, None]
    s = np.arange(c)[None, :]
    mats = [(s <= t), (s > t)]
    for g_big, g_small in _gla_levels(c):
        sub = (t % g_big) // g_small
        ref_q = (t // g_big) * g_big + sub * g_small - 1
        mats.append((sub >= 1) & (s > ref_q) & (s <= t))
        for i in range(1, g_big // g_small):
            ref_k = (t // g_big) * g_big + i * g_small - 1
            mats.append(((t % g_big) < i * g_small) & (s > t) & (s <= ref_k))
    return np.concatenate([m.astype(np.float32) for m in mats], axis=0)


def _gla_kernel(q_ref, k_ref, la_ref, v_ref, gate_ref, s0_ref, ng_ref, dmat_ref,
                out_ref, st_ref, s_ref, *, c):
    ci = pl.program_id(1)
    hk = GLA_HEADS * GLA_DK

    @pl.when(ci == 0)
    def _():
        s_ref[...] = s0_ref[...]

    q, k, la, v = q_ref[...], k_ref[...], la_ref[...], v_ref[...]
    la1 = la.astype(BF16)
    r1 = la - la1.astype(F32)
    la2 = r1.astype(BF16)
    la3 = (r1 - la2.astype(F32)).astype(BF16)
    dmat = dmat_ref[...]
    e_all = (jnp.dot(dmat, la1, preferred_element_type=F32)
             + jnp.dot(dmat, la2, preferred_element_type=F32)
             + jnp.dot(dmat, la3, preferred_element_type=F32))

    def e_blk(i):
        return e_all[i * c:(i + 1) * c, :]

    lane_head = lax.broadcasted_iota(jnp.int32, (c, hk), 1) // GLA_DK
    head_masks = [(lane_head == h).astype(F32) for h in range(GLA_HEADS)]

    def stack_heads(x):
        return jnp.concatenate([x * m for m in head_masks], axis=0).astype(BF16)

    row_t = lax.broadcasted_iota(jnp.int32, (GLA_HEADS * c, c), 0) % c
    col_s = lax.broadcasted_iota(jnp.int32, (GLA_HEADS * c, c), 1)
    row_k = lax.broadcasted_iota(jnp.int32, (c, hk), 0)

    state = s_ref[...]
    o_st = jnp.dot(stack_heads(q * jnp.exp(e_blk(0))), state.astype(BF16),
                   preferred_element_type=F32)

    att = jnp.where(row_t == col_s, _bdot_t(stack_heads(q), k), 0.0)
    idx = 2
    for g_big, g_small in _gla_levels(c):
        q_lvl = stack_heads(q * jnp.exp(e_blk(idx)))
        idx += 1
        same_blk = (row_t // g_big) == (col_s // g_big)
        sub_t = (row_t % g_big) // g_small
        for i in range(1, g_big // g_small):
            k_i = jnp.where((row_k % g_big) < i * g_small, k * jnp.exp(e_blk(idx)), 0.0)
            idx += 1
            p = _bdot_t(q_lvl, k_i)
            att = att + jnp.where(same_blk & (sub_t == i), p, 0.0)

    att = att.astype(BF16)
    outs = []
    for h in range(GLA_HEADS):
        v_h = v[:, h * GLA_DV:(h + 1) * GLA_DV]
        o_h = o_st[h * c:(h + 1) * c, :] + jnp.dot(att[h * c:(h + 1) * c, :], v_h.astype(BF16),
                                                   preferred_element_type=F32)
        o_h = _rms(o_h, ng_ref[:, h * GLA_DV:(h + 1) * GLA_DV])
        outs.append(o_h * _silu(gate_ref[:, h * GLA_DV:(h + 1) * GLA_DV]))
    out_ref[...] = jnp.concatenate(outs, axis=1).astype(out_ref.dtype)

    kd = (k * jnp.exp(e_blk(1))).astype(BF16)
    ones = jnp.ones((c, GLA_DV), BF16)
    b_last = _tdot(la1, ones) + _tdot(la2, ones) + _tdot(la3, ones)
    row_head = lax.broadcasted_iota(jnp.int32, (hk, GLA_DV), 0) // GLA_DK
    upd = jnp.zeros((hk, GLA_DV), F32)
    for h in range(GLA_HEADS):
        kv = _tdot(kd, v[:, h * GLA_DV:(h + 1) * GLA_DV])
        upd = upd + jnp.where(row_head == h, kv, 0.0)
    new_state = jnp.exp(b_last) * state + upd
    s_ref[...] = new_state

    @pl.when(ci == pl.num_programs(1) - 1)
    def _():
        st_ref[...] = new_state


def _gla(q, k, la, v, gate, s0, norm_g, c):
    b, t, hk = q.shape
    dv_all = v.shape[-1]
    dmat = jnp.asarray(_gla_decay_matrices(c), dtype=BF16)
    tok = lambda n: pl.BlockSpec((None, c, n), lambda i, j: (i, j, 0))
    fixed = lambda i, j: (0, 0)
    return pl.pallas_call(
        functools.partial(_gla_kernel, c=c),
        grid=(b, t // c),
        in_specs=[tok(hk), tok(hk), tok(hk), tok(dv_all), tok(dv_all),
                  pl.BlockSpec((None, hk, GLA_DV), lambda i, j: (i, 0, 0)),
                  pl.BlockSpec((1, dv_all), fixed),
                  pl.BlockSpec(dmat.shape, fixed)],
        out_specs=[tok(dv_all), pl.BlockSpec((None, hk, GLA_DV), lambda i, j: (i, 0, 0))],
        out_shape=[jax.ShapeDtypeStruct((b, t, dv_all), BF16),
                   jax.ShapeDtypeStruct((b, hk, GLA_DV), F32)],
        scratch_shapes=[pltpu.VMEM((hk, GLA_DV), F32)],
        compiler_params=_cparams(2),
        name="gla",
    )(q, k, la, v, gate, s0, norm_g.reshape(1, -1), dmat)


def _outproj_kernel(*refs, n_parts):
    x_ref = refs[0]
    a_refs = refs[1:1 + n_parts]
    w_ref = refs[1 + n_parts]
    out_ref = refs[2 + n_parts]
    wbf_ref = refs[3 + n_parts]

    @pl.when(pl.program_id(0) == 0)
    def _():
        wbf_ref[...] = w_ref[...].astype(BF16)

    acc = x_ref[...]
    off = 0
    for a_ref in a_refs:
        n = a_ref.shape[-1]
        acc = acc + jnp.dot(a_ref[...].astype(BF16), wbf_ref[off:off + n, :], preferred_element_type=F32)
        off += n
    out_ref[...] = acc


def _outproj(x, parts, w, tm):
    m = x.shape[0]
    row = lambda i: (i, 0)
    return pl.pallas_call(
        functools.partial(_outproj_kernel, n_parts=len(parts)),
        grid=(m // tm,),
        in_specs=[pl.BlockSpec((tm, D_MODEL), row)]
                 + [pl.BlockSpec((tm, p.shape[-1]), row) for p in parts]
                 + [pl.BlockSpec(w.shape, lambda i: (0, 0))],
        out_specs=pl.BlockSpec((tm, D_MODEL), row),
        out_shape=jax.ShapeDtypeStruct((m, D_MODEL), F32),
        scratch_shapes=[pltpu.VMEM(w.shape, BF16)],
        compiler_params=_cparams(1),
        name="outproj",
    )(x, *parts, w)


def _ffn_kernel(*refs, routed, final_norm):
    it = iter(refs)
    h_ref, g_ref = next(it), next(it)
    rw_ref = next(it) if routed else None
    wg_ref, wu_ref, wd_ref = next(it), next(it), next(it)
    fg_ref = next(it) if final_norm else None
    out_ref, xn_ref, acc_ref = next(it), next(it), next(it)
    gate_ref = next(it) if routed else None
    e, f = pl.program_id(1), pl.program_id(2)

    @pl.when((e == 0) & (f == 0))
    def _():
        xn = _rms(h_ref[...], g_ref[...])
        xn_ref[...] = xn.astype(BF16)
        acc_ref[...] = jnp.zeros_like(acc_ref)
        if routed:
            logits = jnp.dot(xn, rw_ref[...], precision=HIGHEST, preferred_element_type=F32)
            lane = lax.broadcasted_iota(jnp.int32, logits.shape, 1)
            neg = jnp.float32(-jnp.inf)
            logits = jnp.where(lane < N_EXPERTS, logits, neg)
            m1 = jnp.max(logits, axis=-1, keepdims=True)
            i1 = jnp.min(jnp.where(logits == m1, lane, LANES), axis=-1, keepdims=True)
            rest = jnp.where(lane == i1, neg, logits)
            m2 = jnp.max(rest, axis=-1, keepdims=True)
            i2 = jnp.min(jnp.where(rest == m2, lane, LANES), axis=-1, keepdims=True)
            e2 = jnp.exp(m2 - m1)
            g1 = 1.0 / (1.0 + e2)
            g2 = e2 / (1.0 + e2)
            gate_ref[...] = jnp.where(lane == i1, g1, 0.0) + jnp.where(lane == i2, g2, 0.0)

    xn = xn_ref[...]
    hid = _silu(jnp.dot(xn, wg_ref[...].astype(BF16), preferred_element_type=F32)) \
        * jnp.dot(xn, wu_ref[...].astype(BF16), preferred_element_type=F32)
    if routed:
        lane = lax.broadcasted_iota(jnp.int32, gate_ref.shape, 1)
        hid = hid * jnp.sum(jnp.where(lane == e, gate_ref[...], 0.0), axis=-1, keepdims=True)
    acc_ref[...] += jnp.dot(hid.astype(BF16), wd_ref[...].astype(BF16), preferred_element_type=F32)

    @pl.when((e == pl.num_programs(1) - 1) & (f == pl.num_programs(2) - 1))
    def _():
        y = h_ref[...] + acc_ref[...]
        if final_norm:
            y = _rms(y, fg_ref[...])
        out_ref[...] = y


def _ffn(h, g, w_gate, w_up, w_down, tm, tf, router_w=None, final_g=None):
    m = h.shape[0]
    n_e, _, d_ff = w_gate.shape
    routed = router_w is not None
    final_norm = final_g is not None
    row = lambda i, e, f: (i, 0)
    fixed = lambda i, e, f: (0, 0)
    args = [h, g.reshape(1, -1)]
    in_specs = [pl.BlockSpec((tm, D_MODEL), row), pl.BlockSpec((1, D_MODEL), fixed)]
    if routed:
        args.append(jnp.pad(router_w, ((0, 0), (0, LANES - router_w.shape[1]))))
        in_specs.append(pl.BlockSpec((D_MODEL, LANES), fixed))
    args += [w_gate, w_up, w_down]
    in_specs += [pl.BlockSpec((None, D_MODEL, tf), lambda i, e, f: (e, 0, f)),
                 pl.BlockSpec((None, D_MODEL, tf), lambda i, e, f: (e, 0, f)),
                 pl.BlockSpec((None, tf, D_MODEL), lambda i, e, f: (e, f, 0))]
    if final_norm:
        args.append(final_g.reshape(1, -1))
        in_specs.append(pl.BlockSpec((1, D_MODEL), fixed))
    scratch = [pltpu.VMEM((tm, D_MODEL), BF16), pltpu.VMEM((tm, D_MODEL), F32)]
    if routed:
        scratch.append(pltpu.VMEM((tm, LANES), F32))
    return pl.pallas_call(
        functools.partial(_ffn_kernel, routed=routed, final_norm=final_norm),
        grid=(m // tm, n_e, d_ff // tf),
        in_specs=in_specs,
        out_specs=pl.BlockSpec((tm, D_MODEL), row),
        out_shape=jax.ShapeDtypeStruct((m, D_MODEL), F32),
        scratch_shapes=scratch,
        compiler_params=_cparams(3),
        name="moe_ffn" if routed else "ffn",
    )(*args)


def _rope_tables(pos):
    half = HEAD_DIM // 2
    inv = jnp.power(ROPE_THETA, -jnp.arange(half, dtype=F32) / half)
    ang = pos.astype(F32)[:, None] * inv[None, :]
    cos, sin = jnp.cos(ang), jnp.sin(ang)
    cos_t = jnp.concatenate([cos, cos, cos, cos], axis=1)
    sin_t = jnp.concatenate([-sin, sin, -sin, sin], axis=1)
    return cos_t, sin_t


def _rope_apply(y, cos_t, sin_t):
    half = HEAD_DIM // 2
    first = (lax.broadcasted_iota(jnp.int32, (y.shape[0], LANES), 1) % HEAD_DIM) < half
    cols = []
    for c in range(y.shape[1] // LANES):
        yc = y[:, c * LANES:(c + 1) * LANES]
        swapped = jnp.where(first, pltpu.roll(yc, LANES - half, 1), pltpu.roll(yc, half, 1))
        cols.append(yc * cos_t + swapped * sin_t)
    return jnp.concatenate(cols, axis=1)


def _qkv_kernel(x_ref, g_ref, w_ref, cos_ref, sin_ref, q_ref, k_ref, v_ref, wbf_ref, *, head_major):
    @pl.when((pl.program_id(0) == 0) & (pl.program_id(1) == 0))
    def _():
        wbf_ref[...] = w_ref[...].astype(BF16)

    xn = _rms(x_ref[...], g_ref[...]).astype(BF16)
    y = jnp.dot(xn, wbf_ref[...], preferred_element_type=F32)
    cos_t, sin_t = cos_ref[...], sin_ref[...]
    q = _rope_apply(y[:, :D_MODEL], cos_t, sin_t) * (HEAD_DIM ** -0.5)
    k = _rope_apply(y[:, D_MODEL:2 * D_MODEL], cos_t, sin_t)
    v = y[:, 2 * D_MODEL:]
    if head_major:
        tm = q.shape[0]
        for h in range(ATT_HEADS):
            sl = slice(h * HEAD_DIM, (h + 1) * HEAD_DIM)
            q_ref[h] = q[:, sl]
            for p in range(tm // PAGE_SIZE):
                rows = slice(p * PAGE_SIZE, (p + 1) * PAGE_SIZE)
                k_ref[p, h] = k[rows, sl]
                v_ref[p, h] = v[rows, sl]
    else:
        q_ref[...] = q
        k_ref[...] = k
        v_ref[...] = v


def _qkv_rope(x, g, w_qkv, pos, tm, head_major):
    b, t, _ = x.shape
    cos_t, sin_t = _rope_tables(pos)
    fixed = lambda i, j: (0, 0)
    if head_major:
        npg = tm // PAGE_SIZE
        out_specs = [pl.BlockSpec((None, ATT_HEADS, tm, HEAD_DIM), lambda i, j: (i, 0, j, 0)),
                     pl.BlockSpec((None, npg, ATT_HEADS, PAGE_SIZE, HEAD_DIM), lambda i, j: (i, j, 0, 0, 0)),
                     pl.BlockSpec((None, npg, ATT_HEADS, PAGE_SIZE, HEAD_DIM), lambda i, j: (i, j, 0, 0, 0))]
        kv_shape = (b, t // PAGE_SIZE, ATT_HEADS, PAGE_SIZE, HEAD_DIM)
        out_shape = [jax.ShapeDtypeStruct((b, ATT_HEADS, t, HEAD_DIM), F32),
                     jax.ShapeDtypeStruct(kv_shape, F32), jax.ShapeDtypeStruct(kv_shape, F32)]
    else:
        out_specs = [pl.BlockSpec((None, tm, D_MODEL), lambda i, j: (i, j, 0))] * 3
        out_shape = [jax.ShapeDtypeStruct((b, t, D_MODEL), F32)] * 3
    return pl.pallas_call(
        functools.partial(_qkv_kernel, head_major=head_major),
        grid=(b, t // tm),
        in_specs=[pl.BlockSpec((None, tm, D_MODEL), lambda i, j: (i, j, 0)),
                  pl.BlockSpec((1, D_MODEL), fixed),
                  pl.BlockSpec(w_qkv.shape, fixed),
                  pl.BlockSpec((tm, LANES), lambda i, j: (j, 0)),
                  pl.BlockSpec((tm, LANES), lambda i, j: (j, 0))],
        out_specs=out_specs,
        out_shape=out_shape,
        scratch_shapes=[pltpu.VMEM(w_qkv.shape, BF16)],
        compiler_params=_cparams(2),
        name="qkv_rope",
    )(x, g.reshape(1, -1), w_qkv, cos_t, sin_t)


def _top_blocks(gate, valid, n_keep):
    nb = gate.shape[1]
    blk = lax.broadcasted_iota(jnp.int32, gate.shape, 1)
    rank = jnp.zeros(gate.shape, F32)
    for m in range(nb):
        g_m = gate[:, m:m + 1]
        beats = (g_m > gate) | ((g_m == gate) & (m < blk))
        rank = rank + jnp.where(beats & valid[:, m:m + 1], 1.0, 0.0)
    return jnp.where(valid & (rank < n_keep), 1.0, 0.0)


def _moba_prompt_kernel(q_ref, k_ref, v_ref, o_ref, *, t, heads_per_step):
    nb = t // MOBA_BLOCK
    neg = jnp.float32(-jnp.inf)
    outs = []
    for hh in range(heads_per_step):
        q = q_ref[hh]
        k = k_ref[:, hh].reshape(t, HEAD_DIM)
        v = v_ref[:, hh].reshape(t, HEAD_DIM)
        kmean = jnp.concatenate(
            [jnp.mean(k[n * MOBA_BLOCK:(n + 1) * MOBA_BLOCK, :], axis=0, keepdims=True) for n in range(nb)], axis=0)
        gate = lax.dot_general(q, kmean, (((1,), (1,)), ((), ())), precision=HIGHEST,
                               preferred_element_type=F32)
        own_of_q = lax.broadcasted_iota(jnp.int32, (t, nb), 0) // MOBA_BLOCK
        valid = lax.broadcasted_iota(jnp.int32, (t, nb), 1) < own_of_q
        sel = _top_blocks(gate, valid, MOBA_TOPK)
        kb, vb = k.astype(BF16), v.astype(BF16)
        causal = (lax.broadcasted_iota(jnp.int32, (MOBA_BLOCK, MOBA_BLOCK), 1)
                  <= lax.broadcasted_iota(jnp.int32, (MOBA_BLOCK, MOBA_BLOCK), 0))
        o_blocks = []
        for own in range(nb):
            rows = slice(own * MOBA_BLOCK, (own + 1) * MOBA_BLOCK)
            qb = q[rows, :].astype(BF16)
            scores = []
            for n in range(own):
                s = _bdot_t(qb, kb[n * MOBA_BLOCK:(n + 1) * MOBA_BLOCK, :])
                scores.append(jnp.where(sel[rows, n:n + 1] > 0.0, s, neg))
            scores.append(jnp.where(causal, _bdot_t(qb, kb[rows, :]), neg))
            m = scores[0].max(axis=-1, keepdims=True)
            for s in scores[1:]:
                m = jnp.maximum(m, s.max(axis=-1, keepdims=True))
            den = jnp.zeros((MOBA_BLOCK, 1), F32)
            acc = jnp.zeros((MOBA_BLOCK, HEAD_DIM), F32)
            for n, s in enumerate(scores):
                p = jnp.exp(s - m)
                den = den + p.sum(axis=-1, keepdims=True)
                acc = acc + jnp.dot(p.astype(BF16), vb[n * MOBA_BLOCK:(n + 1) * MOBA_BLOCK, :],
                                    preferred_element_type=F32)
            o_blocks.append(acc / den)
        outs.append(jnp.concatenate(o_blocks, axis=0))
    o_ref[...] = jnp.concatenate(outs, axis=1)


def _moba_prompt(q, k_rows, v_rows):
    b, n_h, t, _ = q.shape
    hps = LANES // HEAD_DIM
    npg = t // PAGE_SIZE
    kv_spec = pl.BlockSpec((None, npg, hps, PAGE_SIZE, HEAD_DIM), lambda i, j: (i, 0, j, 0, 0))
    return pl.pallas_call(
        functools.partial(_moba_prompt_kernel, t=t, heads_per_step=hps),
        grid=(b, n_h // hps),
        in_specs=[pl.BlockSpec((None, hps, t, HEAD_DIM), lambda i, j: (i, j, 0, 0)), kv_spec, kv_spec],
        out_specs=pl.BlockSpec((None, t, LANES), lambda i, j: (i, 0, j)),
        out_shape=jax.ShapeDtypeStruct((b, t, n_h * HEAD_DIM), F32),
        compiler_params=_cparams(2),
        name="moba_prompt",
    )(q, k_rows, v_rows)


def _kmean_kernel(pt_ref, *refs, pages_per_step):
    page_refs = refs[:pages_per_step]
    out_ref = refs[pages_per_step]
    ppb = MOBA_BLOCK // PAGE_SIZE
    for blk in range(pages_per_step // ppb):
        tot = page_refs[blk * ppb][...].sum(axis=1)
        for i in range(1, ppb):
            tot = tot + page_refs[blk * ppb + i][...].sum(axis=1)
        out_ref[blk] = tot * (1.0 / MOBA_BLOCK)


def _sample_kmean(cache_k, page_table, n_blocks, pages_per_step=4):
    db = page_table.shape[0]
    n_pages = page_table.shape[1]
    ppb = MOBA_BLOCK // PAGE_SIZE
    steps = n_blocks * ppb // pages_per_step

    def page_spec(i):
        return pl.BlockSpec((None, ATT_HEADS, PAGE_SIZE, HEAD_DIM),
                            lambda b, s, pt: (pt[b * n_pages + s * pages_per_step + i], 0, 0, 0))

    grid_spec = pltpu.PrefetchScalarGridSpec(
        num_scalar_prefetch=1,
        grid=(db, steps),
        in_specs=[page_spec(i) for i in range(pages_per_step)],
        out_specs=pl.BlockSpec((None, pages_per_step // ppb, ATT_HEADS, HEAD_DIM), lambda b, s, pt: (b, s, 0, 0)),
    )
    return pl.pallas_call(
        functools.partial(_kmean_kernel, pages_per_step=pages_per_step),
        grid_spec=grid_spec,
        out_shape=jax.ShapeDtypeStruct((db, n_blocks, ATT_HEADS, HEAD_DIM), F32),
        compiler_params=_cparams(2),
        name="sample_kmean",
    )(page_table.reshape(-1), *([cache_k] * pages_per_step))


def _select_kernel(q_ref, kmean_ref, sel_ref, *, n_q):
    n_blk = kmean_ref.shape[1]
    q = q_ref[...]
    row_head = lax.broadcasted_iota(jnp.int32, (q.shape[0], n_blk), 0) // n_q
    gate = jnp.zeros((q.shape[0], n_blk), F32)
    for h in range(ATT_HEADS):
        g_h = lax.dot_general(q, kmean_ref[h], (((1,), (1,)), ((), ())),
                              precision=HIGHEST, preferred_element_type=F32)
        gate = jnp.where(row_head == h, g_h, gate)
    lane = lax.broadcasted_iota(jnp.int32, gate.shape, 1)
    out_lane = lax.broadcasted_iota(jnp.int32, (gate.shape[0], LANES), 1)
    sel = jnp.zeros((gate.shape[0], LANES), jnp.int32)
    neg = jnp.float32(-jnp.inf)
    for r in range(MOBA_TOPK):
        m = jnp.max(gate, axis=-1, keepdims=True)
        idx = jnp.min(jnp.where(gate == m, lane, n_blk), axis=-1, keepdims=True)
        sel = jnp.where(out_lane == r, idx, sel)
        gate = jnp.where(lane == idx, neg, gate)
    sel_ref[...] = sel


def _sample_select(q, kmean):
    db, n_h, n_q, _ = q.shape
    n_blk = kmean.shape[2]
    return pl.pallas_call(
        functools.partial(_select_kernel, n_q=n_q),
        grid=(db,),
        in_specs=[pl.BlockSpec((None, n_h * n_q, HEAD_DIM), lambda b: (b, 0, 0)),
                  pl.BlockSpec((None, n_h, n_blk, HEAD_DIM), lambda b: (b, 0, 0, 0))],
        out_specs=pl.BlockSpec((None, n_h * n_q, LANES), lambda b: (b, 0, 0)),
        out_shape=jax.ShapeDtypeStruct((db, n_h * n_q, LANES), jnp.int32),
        compiler_params=_cparams(1),
        name="sample_select",
    )(q.reshape(db, n_h * n_q, HEAD_DIM), kmean)


def _sample_attn_kernel(pg_ref, q_ref, kn_ref, vn_ref, *refs, n_q, tiles_per_q):
    n_tiles = n_q * tiles_per_q
    k_refs = refs[:n_tiles]
    v_refs = refs[n_tiles:2 * n_tiles]
    o_ref = refs[2 * n_tiles]
    neg = jnp.float32(-jnp.inf)
    q = q_ref[...]
    kn, vn = kn_ref[...], vn_ref[...]
    rows8 = 8
    q8 = jnp.concatenate([q] * (rows8 // n_q), axis=0)
    kcat = jnp.concatenate([r[...] for r in k_refs], axis=0)
    vcat = jnp.concatenate([r[...] for r in v_refs], axis=0)
    s = _bdot_t(q8, kcat)
    row = lax.broadcasted_iota(jnp.int32, s.shape, 0) % n_q
    col = lax.broadcasted_iota(jnp.int32, s.shape, 1)
    s = jnp.where((col // (tiles_per_q * PAGE_SIZE)) == row, s, neg)
    row_o = lax.broadcasted_iota(jnp.int32, (rows8, 1), 0) % n_q
    s_own = [jnp.where(row_o >= c, jnp.sum(q8 * kn[c:c + 1, :], axis=-1, keepdims=True), neg)
             for c in range(n_q)]
    m = s.max(axis=-1, keepdims=True)
    for s_c in s_own:
        m = jnp.maximum(m, s_c)
    p = jnp.exp(s - m)
    den = p.sum(axis=-1, keepdims=True)
    acc = _bdot(p, vcat)
    for c, s_c in enumerate(s_own):
        p_c = jnp.exp(s_c - m)
        den = den + p_c
        acc = acc + p_c * vn[c:c + 1, :]
    o_ref[...] = (acc / den)[:n_q, :]


def _sample_attn(q, k_new, v_new, cache_k, cache_v, pages):
    db, n_h, n_q, _ = q.shape
    tiles_per_q = MOBA_TOPK * (MOBA_BLOCK // PAGE_SIZE)
    n_tiles = n_q * tiles_per_q

    def tile_spec(i):
        return pl.BlockSpec((None, None, PAGE_SIZE, HEAD_DIM),
                            lambda b, h, pg: (pg[(b * n_h + h) * n_tiles + i], h, 0, 0))

    small = pl.BlockSpec((None, None, n_q, HEAD_DIM), lambda b, h, pg: (b, h, 0, 0))
    grid_spec = pltpu.PrefetchScalarGridSpec(
        num_scalar_prefetch=1,
        grid=(db, n_h),
        in_specs=[small, small, small] + [tile_spec(i) for i in range(n_tiles)] * 2,
        out_specs=small,
    )
    return pl.pallas_call(
        functools.partial(_sample_attn_kernel, n_q=n_q, tiles_per_q=tiles_per_q),
        grid_spec=grid_spec,
        out_shape=jax.ShapeDtypeStruct((db, n_h, n_q, HEAD_DIM), F32),
        compiler_params=_cparams(2),
        name="sample_attn",
    )(pages, q, k_new, v_new, *([cache_k] * n_tiles), *([cache_v] * n_tiles))


def _even_layer(x, conv_prev, gla_prev, p, tm, conv_tt, gla_c, ffn_tm, ffn_tf):
    b, t, d = x.shape
    xf = x.reshape(b * t, d)
    glu, q, k, v, gate, la = _inproj(xf, p["norm_mix"], p["w_in"], p["w_a2"], p["b_a"], tm)
    as_bt = lambda a: a.reshape(b, t, a.shape[-1])
    conv_out, conv_state = _conv_module(as_bt(glu), conv_prev, p["conv_w"], p["conv_b"], p["ln_g"], p["ln_b"], conv_tt)
    o, gla_state = _gla(as_bt(q), as_bt(k), as_bt(la), as_bt(v), as_bt(gate),
                        gla_prev.reshape(b, GLA_HEADS * GLA_DK, GLA_DV), p["gla_norm_g"], gla_c)
    h = _outproj(xf, [conv_out.reshape(b * t, -1), o.reshape(b * t, -1)], p["w_out"], tm)
    h = _ffn(h, p["norm_ffn"], p["ffn_w_gate"], p["ffn_w_up"], p["ffn_w_down"], ffn_tm, ffn_tf)
    return h.reshape(b, t, d), conv_state, gla_state.reshape(b, GLA_HEADS, GLA_DK, GLA_DV)


def kernel(x_prompt, x_sample, state_conv, state_gla, cache_k, cache_v, page_table, norm_mix_even, w_in_even, conv_w, conv_b, conv_ln_g, conv_ln_b, gla_w_a2, gla_b_a, gla_norm_g, w_out_even, norm_ffn_even, ffn_w_gate, ffn_w_up, ffn_w_down, norm_mix_odd, w_qkv_odd, w_out_odd, norm_ffn_odd, router_w, moe_w_gate, moe_w_up, moe_w_down, final_norm_g):
    bp, tp, d = x_prompt.shape
    db, ts, _ = x_sample.shape
    even = dict(norm_mix=norm_mix_even[0], w_in=w_in_even[0], conv_w=conv_w[0], conv_b=conv_b[0],
                ln_g=conv_ln_g[0], ln_b=conv_ln_b[0], w_a2=gla_w_a2[0], b_a=gla_b_a[0],
                gla_norm_g=gla_norm_g[0], w_out=w_out_even[0], norm_ffn=norm_ffn_even[0],
                ffn_w_gate=ffn_w_gate, ffn_w_up=ffn_w_up, ffn_w_down=ffn_w_down)

    hp, conv_p, gla_p = _even_layer(
        x_prompt, jnp.zeros((bp, CONV_W - 1, D_CONV), F32), jnp.zeros((bp, GLA_HEADS, GLA_DK, GLA_DV), F32),
        even, tm=512, conv_tt=256, gla_c=GLA_CHUNK, ffn_tm=1024, ffn_tf=256)

    m_s = db * ts
    xs = x_sample.reshape(m_s, d)
    glu, q, k, v, gate, la = _inproj(xs, even["norm_mix"], even["w_in"], even["w_a2"], even["b_a"], m_s)
    as_bt = lambda a: a.reshape(db, ts, a.shape[-1])
    conv_out_s, conv_s = _conv_module(as_bt(glu), state_conv[0], even["conv_w"], even["conv_b"],
                                      even["ln_g"], even["ln_b"], ts)
    pad_t = 16
    pad8 = lambda a: jnp.pad(as_bt(a), ((0, 0), (0, pad_t - ts), (0, 0)))
    o_s, gla_s = _gla(pad8(q), pad8(k), pad8(la), pad8(v), pad8(gate),
                      state_gla[0].reshape(db, GLA_HEADS * GLA_DK, GLA_DV), even["gla_norm_g"], pad_t)
    hs = _outproj(xs, [conv_out_s.reshape(m_s, -1), o_s[:, :ts].reshape(m_s, -1)], even["w_out"], m_s)
    hs = _ffn(hs, even["norm_ffn"], ffn_w_gate, ffn_w_up, ffn_w_down, m_s, 256)
    gla_s = gla_s.reshape(db, GLA_HEADS, GLA_DK, GLA_DV)

    qp, k_rows, v_rows = _qkv_rope(hp, norm_mix_odd[0], w_qkv_odd[0], jnp.arange(tp), 256, True)
    op = _moba_prompt(qp, k_rows, v_rows)
    hp2 = _outproj(hp.reshape(bp * tp, d), [op.reshape(bp * tp, d)], w_out_odd[0], 512)
    y_prompt = _ffn(hp2, norm_ffn_odd[0], moe_w_gate[0], moe_w_up[0], moe_w_down[0], 1024, 512,
                    router_w=router_w[0], final_g=final_norm_g).reshape(bp, tp, d)

    n_blocks = page_table.shape[1] * PAGE_SIZE // MOBA_BLOCK
    past_len = page_table.shape[1] * PAGE_SIZE
    pos_s = past_len + (jnp.arange(m_s) % ts)
    qs, ks, vs = _qkv_rope(hs.reshape(1, m_s, d), norm_mix_odd[0], w_qkv_odd[0], pos_s, m_s, False)
    to_heads = lambda a: a.reshape(db, ts, ATT_HEADS, HEAD_DIM).transpose(0, 2, 1, 3)
    qs, ks, vs = to_heads(qs), to_heads(ks), to_heads(vs)
    kmean = _sample_kmean(cache_k[0], page_table, n_blocks).transpose(0, 2, 1, 3)
    sel = _sample_select(qs, kmean)[:, :, :MOBA_TOPK]
    ppb = MOBA_BLOCK // PAGE_SIZE
    logical = (sel[..., None] * ppb + jnp.arange(ppb)).reshape(db, -1)
    pages = jnp.take_along_axis(page_table, logical, axis=1).reshape(-1)
    os_ = _sample_attn(qs, ks, vs, cache_k[0], cache_v[0], pages)
    os_ = os_.transpose(0, 2, 1, 3).reshape(m_s, d)
    hs2 = _outproj(hs, [os_], w_out_odd[0], m_s)
    y_sample = _ffn(hs2, norm_ffn_odd[0], moe_w_gate[0], moe_w_up[0], moe_w_down[0], m_s, 512,
                    router_w=router_w[0], final_g=final_norm_g).reshape(db, ts, d)

    return (y_prompt, y_sample, conv_p[None], gla_p[None], k_rows[None], v_rows[None],
            conv_s[None], gla_s[None], ks[None], vs[None])
```

```python
import functools

import jax
import jax.numpy as jnp
import numpy as np
from jax import lax
from jax.experimental import pallas as pl
from jax.experimental.pallas import tpu as pltpu

F32 = jnp.float32
BF16 = jnp.bfloat16
HIGHEST = lax.Precision.HIGHEST

LANES = 128
NORM_EPS = 1e-6
D_MODEL = 1024
D_CONV = 512
CONV_W = 31
CONV_HALO = 32
GLA_HEADS = 4
GLA_DK = 64
GLA_DV = 128
GLA_TAU = 16.0
GLA_CHUNK = 64
ATT_HEADS = 16
HEAD_DIM = 64
MOBA_BLOCK = 256
MOBA_TOPK = 3
PAGE_SIZE = 128
ROPE_THETA = 10000.0
N_EXPERTS = 8
VMEM_LIMIT = 56 * 1024 * 1024


def _cparams(n_axes):
    return pltpu.CompilerParams(dimension_semantics=("arbitrary",) * n_axes,
                                vmem_limit_bytes=VMEM_LIMIT)


def _rms(x, g):
    return x * lax.rsqrt(jnp.mean(x * x, axis=-1, keepdims=True) + NORM_EPS) * g


def _silu(x):
    return x * jax.nn.sigmoid(x)


def _bdot(a, b):
    return jnp.dot(a.astype(BF16), b.astype(BF16), preferred_element_type=F32)


def _bdot_t(a, b):
    return lax.dot_general(a.astype(BF16), b.astype(BF16), (((1,), (1,)), ((), ())),
                           preferred_element_type=F32)


def _tdot(a, b):
    return lax.dot_general(a.astype(BF16), b.astype(BF16), (((0,), (0,)), ((), ())),
                           preferred_element_type=F32)


def _inproj_kernel(x_ref, g_ref, w_ref, wa1_ref, wa2_ref, ba_ref,
                   glu_ref, q_ref, k_ref, v_ref, gate_ref, la_ref, wbf_ref, wa1bf_ref):
    @pl.when(pl.program_id(0) == 0)
    def _():
        wbf_ref[...] = w_ref[...].astype(BF16)
        wa1bf_ref[...] = wa1_ref[...].astype(BF16)

    xn = _rms(x_ref[...], g_ref[...]).astype(BF16)
    y = jnp.dot(xn, wbf_ref[...], preferred_element_type=F32)
    glu_ref[...] = y[:, :D_CONV] * jax.nn.sigmoid(y[:, D_CONV:2 * D_CONV])
    q_ref[...] = y[:, 1024:1280] * (GLA_DK ** -0.5)
    k_ref[...] = y[:, 1280:1536]
    v_ref[...] = y[:, 1536:2048]
    gate_ref[...] = y[:, 2048:2560]
    a = jnp.dot(xn, wa1bf_ref[...], preferred_element_type=F32)
    z = jnp.dot(a, wa2_ref[...], precision=HIGHEST, preferred_element_type=F32) + ba_ref[...]
    log_sig = jnp.minimum(z, 0.0) - jnp.log(1.0 + jnp.exp(-jnp.abs(z)))
    la_ref[...] = log_sig * (1.0 / GLA_TAU)


def _inproj(x, g, w_in, w_a2, b_a, tm):
    m = x.shape[0]
    n_main = 2560
    wa1 = jnp.pad(w_in[:, n_main:], ((0, 0), (0, LANES - (w_in.shape[1] - n_main))))
    wa2 = jnp.pad(w_a2, ((0, LANES - w_a2.shape[0]), (0, 0)))
    row = lambda i: (i, 0)
    fixed = lambda i: (0, 0)
    outs = [(D_CONV, F32), (256, F32), (256, F32), (512, F32), (512, F32), (256, F32)]
    return pl.pallas_call(
        _inproj_kernel,
        grid=(m // tm,),
        in_specs=[pl.BlockSpec((tm, D_MODEL), row),
                  pl.BlockSpec((1, D_MODEL), fixed),
                  pl.BlockSpec((D_MODEL, n_main), fixed),
                  pl.BlockSpec((D_MODEL, LANES), fixed),
                  pl.BlockSpec((LANES, 256), fixed),
                  pl.BlockSpec((1, 256), fixed)],
        out_specs=[pl.BlockSpec((tm, n), row) for n, _ in outs],
        out_shape=[jax.ShapeDtypeStruct((m, n), dt) for n, dt in outs],
        scratch_shapes=[pltpu.VMEM((D_MODEL, n_main), BF16), pltpu.VMEM((D_MODEL, LANES), BF16)],
        compiler_params=_cparams(1),
        name="inproj",
    )(x, g.reshape(1, -1), w_in, wa1, wa2, b_a.reshape(1, -1))


def _conv_kernel(glu_ref, prev_ref, w_ref, b_ref, lg_ref, lb_ref, out_ref, st_ref, full_ref, *, tt, rc):
    t = pl.program_id(1)
    lo = CONV_HALO - (CONV_W - 1)

    @pl.when(t == 0)
    def _():
        full_ref[0:lo, :] = jnp.zeros((lo, D_CONV), F32)
        full_ref[lo:CONV_HALO, :] = prev_ref[...]

    full_ref[CONV_HALO:CONV_HALO + tt, :] = glu_ref[...]
    for c in range(tt // rc):
        base = c * rc + lo
        acc = full_ref[base:base + rc, :] * w_ref[0:1, :]
        for j in range(1, CONV_W):
            acc = acc + full_ref[base + j:base + j + rc, :] * w_ref[j:j + 1, :]
        y = acc + b_ref[...]
        mu = jnp.mean(y, axis=-1, keepdims=True)
        d = y - mu
        var = jnp.mean(d * d, axis=-1, keepdims=True)
        yn = d * lax.rsqrt(var + NORM_EPS) * lg_ref[...] + lb_ref[...]
        out_ref[c * rc:(c + 1) * rc, :] = _silu(yn).astype(out_ref.dtype)

    tail = full_ref[tt + lo:tt + CONV_HALO, :]

    @pl.when(t == pl.num_programs(1) - 1)
    def _():
        st_ref[...] = tail

    full_ref[lo:CONV_HALO, :] = tail


def _conv_module(glu, prev, conv_w, conv_b, ln_g, ln_b, tt):
    b, t, _ = glu.shape
    rc = min(tt, 32)
    fixed = lambda i, j: (0, 0)
    return pl.pallas_call(
        functools.partial(_conv_kernel, tt=tt, rc=rc),
        grid=(b, t // tt),
        in_specs=[pl.BlockSpec((None, tt, D_CONV), lambda i, j: (i, j, 0)),
                  pl.BlockSpec((None, CONV_W - 1, D_CONV), lambda i, j: (i, 0, 0)),
                  pl.BlockSpec((CONV_W, D_CONV), fixed),
                  pl.BlockSpec((1, D_CONV), fixed),
                  pl.BlockSpec((1, D_CONV), fixed),
                  pl.BlockSpec((1, D_CONV), fixed)],
        out_specs=[pl.BlockSpec((None, tt, D_CONV), lambda i, j: (i, j, 0)),
                   pl.BlockSpec((None, CONV_W - 1, D_CONV), lambda i, j: (i, 0, 0))],
        out_shape=[jax.ShapeDtypeStruct((b, t, D_CONV), BF16),
                   jax.ShapeDtypeStruct((b, CONV_W - 1, D_CONV), F32)],
        scratch_shapes=[pltpu.VMEM((CONV_HALO + tt, D_CONV), F32)],
        compiler_params=_cparams(2),
        name="conv_module",
    )(glu, prev, conv_w, conv_b.reshape(1, -1), ln_g.reshape(1, -1), ln_b.reshape(1, -1))


def _gla_levels(c):
    levels, g_big = [], c
    while g_big > 1:
        g_small = max(g_big // 4, 1)
        levels.append((g_big, g_small))
        g_big = g_small
    return levels


def _gla_decay_matrices(c):
    t = np.arange(c)[:, None]
    s = np.arange(c)[None, :]
    mats = [(s <= t), (s > t)]
    for g_big, g_small in _gla_levels(c):
        sub = (t % g_big) // g_small
        ref_q = (t // g_big) * g_big + sub * g_small - 1
        mats.append((sub >= 1) & (s > ref_q) & (s <= t))
        for i in range(1, g_big // g_small):
            ref_k = (t // g_big) * g_big + i * g_small - 1
            mats.append(((t % g_big) < i * g_small) & (s > t) & (s <= ref_k))
    return np.concatenate([m.astype(np.float32) for m in mats], axis=0)


def _gla_kernel(q_ref, k_ref, la_ref, v_ref, gate_ref, s0_ref, ng_ref, dmat_ref,
                out_ref, st_ref, s_ref, *, c):
    ci = pl.program_id(1)
    hk = GLA_HEADS * GLA_DK

    @pl.when(ci == 0)
    def _():
        s_ref[...] = s0_ref[...]

    q, k, la, v = q_ref[...], k_ref[...], la_ref[...], v_ref[...]
    la1 = la.astype(BF16)
    r1 = la - la1.astype(F32)
    la2 = r1.astype(BF16)
    la3 = (r1 - la2.astype(F32)).astype(BF16)
    dmat = dmat_ref[...]
    e_all = (jnp.dot(dmat, la1, preferred_element_type=F32)
             + jnp.dot(dmat, la2, preferred_element_type=F32)
             + jnp.dot(dmat, la3, preferred_element_type=F32))

    def e_blk(i):
        return e_all[i * c:(i + 1) * c, :]

    lane_head = lax.broadcasted_iota(jnp.int32, (c, hk), 1) // GLA_DK
    head_masks = [(lane_head == h).astype(F32) for h in range(GLA_HEADS)]

    def stack_heads(x):
        return jnp.concatenate([x * m for m in head_masks], axis=0).astype(BF16)

    row_t = lax.broadcasted_iota(jnp.int32, (GLA_HEADS * c, c), 0) % c
    col_s = lax.broadcasted_iota(jnp.int32, (GLA_HEADS * c, c), 1)
    row_k = lax.broadcasted_iota(jnp.int32, (c, hk), 0)

    state = s_ref[...]
    o_st = jnp.dot(stack_heads(q * jnp.exp(e_blk(0))), state.astype(BF16),
                   preferred_element_type=F32)

    att = jnp.where(row_t == col_s, _bdot_t(stack_heads(q), k), 0.0)
    idx = 2
    for g_big, g_small in _gla_levels(c):
        q_lvl = stack_heads(q * jnp.exp(e_blk(idx)))
        idx += 1
        same_blk = (row_t // g_big) == (col_s // g_big)
        sub_t = (row_t % g_big) // g_small
        for i in range(1, g_big // g_small):
            k_i = jnp.where((row_k % g_big) < i * g_small, k * jnp.exp(e_blk(idx)), 0.0)
            idx += 1
            p = _bdot_t(q_lvl, k_i)
            att = att + jnp.where(same_blk & (sub_t == i), p, 0.0)

    att = att.astype(BF16)
    outs = []
    for h in range(GLA_HEADS):
        v_h = v[:, h * GLA_DV:(h + 1) * GLA_DV]
        o_h = o_st[h * c:(h + 1) * c, :] + jnp.dot(att[h * c:(h + 1) * c, :], v_h.astype(BF16),
                                                   preferred_element_type=F32)
        o_h = _rms(o_h, ng_ref[:, h * GLA_DV:(h + 1) * GLA_DV])
        outs.append(o_h * _silu(gate_ref[:, h * GLA_DV:(h + 1) * GLA_DV]))
    out_ref[...] = jnp.concatenate(outs, axis=1).astype(out_ref.dtype)

    kd = (k * jnp.exp(e_blk(1))).astype(BF16)
    ones = jnp.ones((c, GLA_DV), BF16)
    b_last = _tdot(la1, ones) + _tdot(la2, ones) + _tdot(la3, ones)
    row_head = lax.broadcasted_iota(jnp.int32, (hk, GLA_DV), 0) // GLA_DK
    upd = jnp.zeros((hk, GLA_DV), F32)
    for h in range(GLA_HEADS):
        kv = _tdot(kd, v[:, h * GLA_DV:(h + 1) * GLA_DV])
        upd = upd + jnp.where(row_head == h, kv, 0.0)
    new_state = jnp.exp(b_last) * state + upd
    s_ref[...] = new_state

    @pl.when(ci == pl.num_programs(1) - 1)
    def _():
        st_ref[...] = new_state


def _gla(q, k, la, v, gate, s0, norm_g, c):
    b, t, hk = q.shape
    dv_all = v.shape[-1]
    dmat = jnp.asarray(_gla_decay_matrices(c), dtype=BF16)
    tok = lambda n: pl.BlockSpec((None, c, n), lambda i, j: (i, j, 0))
    fixed = lambda i, j: (0, 0)
    return pl.pallas_call(
        functools.partial(_gla_kernel, c=c),
        grid=(b, t // c),
        in_specs=[tok(hk), tok(hk), tok(hk), tok(dv_all), tok(dv_all),
                  pl.BlockSpec((None, hk, GLA_DV), lambda i, j: (i, 0, 0)),
                  pl.BlockSpec((1, dv_all), fixed),
                  pl.BlockSpec(dmat.shape, fixed)],
        out_specs=[tok(dv_all), pl.BlockSpec((None, hk, GLA_DV), lambda i, j: (i, 0, 0))],
        out_shape=[jax.ShapeDtypeStruct((b, t, dv_all), BF16),
                   jax.ShapeDtypeStruct((b, hk, GLA_DV), F32)],
        scratch_shapes=[pltpu.VMEM((hk, GLA_DV), F32)],
        compiler_params=_cparams(2),
        name="gla",
    )(q, k, la, v, gate, s0, norm_g.reshape(1, -1), dmat)


def _outproj_kernel(*refs, n_parts):
    x_ref = refs[0]
    a_refs = refs[1:1 + n_parts]
    w_ref = refs[1 + n_parts]
    out_ref = refs[2 + n_parts]
    wbf_ref = refs[3 + n_parts]

    @pl.when(pl.program_id(0) == 0)
    def _():
        wbf_ref[...] = w_ref[...].astype(BF16)

    acc = x_ref[...]
    off = 0
    for a_ref in a_refs:
        n = a_ref.shape[-1]
        acc = acc + jnp.dot(a_ref[...].astype(BF16), wbf_ref[off:off + n, :], preferred_element_type=F32)
        off += n
    out_ref[...] = acc


def _outproj(x, parts, w, tm):
    m = x.shape[0]
    row = lambda i: (i, 0)
    return pl.pallas_call(
        functools.partial(_outproj_kernel, n_parts=len(parts)),
        grid=(m // tm,),
        in_specs=[pl.BlockSpec((tm, D_MODEL), row)]
                 + [pl.BlockSpec((tm, p.shape[-1]), row) for p in parts]
                 + [pl.BlockSpec(w.shape, lambda i: (0, 0))],
        out_specs=pl.BlockSpec((tm, D_MODEL), row),
        out_shape=jax.ShapeDtypeStruct((m, D_MODEL), F32),
        scratch_shapes=[pltpu.VMEM(w.shape, BF16)],
        compiler_params=_cparams(1),
        name="outproj",
    )(x, *parts, w)


def _ffn_kernel(*refs, routed, final_norm):
    it = iter(refs)
    h_ref, g_ref = next(it), next(it)
    rw_ref = next(it) if routed else None
    wg_ref, wu_ref, wd_ref = next(it), next(it), next(it)
    fg_ref = next(it) if final_norm else None
    out_ref, xn_ref, acc_ref = next(it), next(it), next(it)
    gate_ref = next(it) if routed else None
    e, f = pl.program_id(1), pl.program_id(2)

    @pl.when((e == 0) & (f == 0))
    def _():
        xn = _rms(h_ref[...], g_ref[...])
        xn_ref[...] = xn.astype(BF16)
        acc_ref[...] = jnp.zeros_like(acc_ref)
        if routed:
            gate_ref[...] = _top2_gates(jnp.dot(xn, rw_ref[...], precision=HIGHEST, preferred_element_type=F32))

    xn = xn_ref[...]
    hid = _silu(jnp.dot(xn, wg_ref[...].astype(BF16), preferred_element_type=F32)) \
        * jnp.dot(xn, wu_ref[...].astype(BF16), preferred_element_type=F32)
    if routed:
        lane = lax.broadcasted_iota(jnp.int32, gate_ref.shape, 1)
        hid = hid * jnp.sum(jnp.where(lane == e, gate_ref[...], 0.0), axis=-1, keepdims=True)
    acc_ref[...] += jnp.dot(hid.astype(BF16), wd_ref[...].astype(BF16), preferred_element_type=F32)

    @pl.when((e == pl.num_programs(1) - 1) & (f == pl.num_programs(2) - 1))
    def _():
        y = h_ref[...] + acc_ref[...]
        if final_norm:
            y = _rms(y, fg_ref[...])
        out_ref[...] = y


def _ffn(h, g, w_gate, w_up, w_down, tm, tf, router_w=None, final_g=None):
    m = h.shape[0]
    n_e, _, d_ff = w_gate.shape
    routed = router_w is not None
    final_norm = final_g is not None
    row = lambda i, e, f: (i, 0)
    fixed = lambda i, e, f: (0, 0)
    args = [h, g.reshape(1, -1)]
    in_specs = [pl.BlockSpec((tm, D_MODEL), row), pl.BlockSpec((1, D_MODEL), fixed)]
    if routed:
        args.append(jnp.pad(router_w, ((0, 0), (0, LANES - router_w.shape[1]))))
        in_specs.append(pl.BlockSpec((D_MODEL, LANES), fixed))
    args += [w_gate, w_up, w_down]
    in_specs += [pl.BlockSpec((None, D_MODEL, tf), lambda i, e, f: (e, 0, f)),
                 pl.BlockSpec((None, D_MODEL, tf), lambda i, e, f: (e, 0, f)),
                 pl.BlockSpec((None, tf, D_MODEL), lambda i, e, f: (e, f, 0))]
    if final_norm:
        args.append(final_g.reshape(1, -1))
        in_specs.append(pl.BlockSpec((1, D_MODEL), fixed))
    scratch = [pltpu.VMEM((tm, D_MODEL), BF16), pltpu.VMEM((tm, D_MODEL), F32)]
    if routed:
        scratch.append(pltpu.VMEM((tm, LANES), F32))
    return pl.pallas_call(
        functools.partial(_ffn_kernel, routed=routed, final_norm=final_norm),
        grid=(m // tm, n_e, d_ff // tf),
        in_specs=in_specs,
        out_specs=pl.BlockSpec((tm, D_MODEL), row),
        out_shape=jax.ShapeDtypeStruct((m, D_MODEL), F32),
        scratch_shapes=scratch,
        compiler_params=_cparams(3),
        name="moe_ffn" if routed else "ffn",
    )(*args)


def _top2_gates(logits):
    lane = lax.broadcasted_iota(jnp.int32, logits.shape, 1)
    neg = jnp.float32(-jnp.inf)
    logits = jnp.where(lane < N_EXPERTS, logits, neg)
    m1 = jnp.max(logits, axis=-1, keepdims=True)
    i1 = jnp.min(jnp.where(logits == m1, lane, LANES), axis=-1, keepdims=True)
    rest = jnp.where(lane == i1, neg, logits)
    m2 = jnp.max(rest, axis=-1, keepdims=True)
    i2 = jnp.min(jnp.where(rest == m2, lane, LANES), axis=-1, keepdims=True)
    e2 = jnp.exp(m2 - m1)
    g1 = 1.0 / (1.0 + e2)
    g2 = e2 / (1.0 + e2)
    return jnp.where(lane == i1, g1, 0.0) + jnp.where(lane == i2, g2, 0.0)


def _route_kernel(x_ref, a_ref, w_ref, g_ref, rw_ref, h_ref, xn_ref, gate_ref, dest_ref, wbf_ref, ltri_ref):
    tm = x_ref.shape[0]

    @pl.when(pl.program_id(0) == 0)
    def _():
        wbf_ref[...] = w_ref[...].astype(BF16)
        earlier = (lax.broadcasted_iota(jnp.int32, (tm, tm), 1) < lax.broadcasted_iota(jnp.int32, (tm, tm), 0))
        ltri_ref[...] = jnp.where(earlier, 1.0, 0.0).astype(BF16)

    h = x_ref[...] + jnp.dot(a_ref[...].astype(BF16), wbf_ref[...], preferred_element_type=F32)
    h_ref[...] = h
    xn = _rms(h, g_ref[...])
    xn_ref[...] = xn.astype(BF16)
    gate = _top2_gates(jnp.dot(xn, rw_ref[...], precision=HIGHEST, preferred_element_type=F32))
    gate_ref[...] = gate
    routed = gate > 0.0
    rank = jnp.dot(ltri_ref[...], jnp.where(routed, 1.0, 0.0).astype(BF16), preferred_element_type=F32)
    dest_ref[...] = jnp.where(routed, rank, -1.0)


def _route(x, a, w_out, g, router_w, tm):
    m = x.shape[0]
    row = lambda i: (i, 0)
    fixed = lambda i: (0, 0)
    rw = jnp.pad(router_w, ((0, 0), (0, LANES - router_w.shape[1])))
    return pl.pallas_call(
        _route_kernel,
        grid=(m // tm,),
        in_specs=[pl.BlockSpec((tm, D_MODEL), row), pl.BlockSpec((tm, D_MODEL), row),
                  pl.BlockSpec(w_out.shape, fixed), pl.BlockSpec((1, D_MODEL), fixed),
                  pl.BlockSpec((D_MODEL, LANES), fixed)],
        out_specs=[pl.BlockSpec((tm, D_MODEL), row), pl.BlockSpec((tm, D_MODEL), row),
                   pl.BlockSpec((tm, LANES), row), pl.BlockSpec((tm, LANES), row)],
        out_shape=[jax.ShapeDtypeStruct((m, D_MODEL), F32), jax.ShapeDtypeStruct((m, D_MODEL), BF16),
                   jax.ShapeDtypeStruct((m, LANES), F32), jax.ShapeDtypeStruct((m, LANES), F32)],
        scratch_shapes=[pltpu.VMEM(w_out.shape, BF16), pltpu.VMEM((tm, tm), BF16)],
        compiler_params=_cparams(1),
        name="route",
    )(x, a, w_out, g.reshape(1, -1), rw)


def _experts_kernel(cnt_ref, xn_ref, gate_ref, dest_ref, h_ref, wg_ref, wu_ref, wd_ref, fg_ref, out_ref,
                    dest_t_ref, xc_ref, yc_ref, acc_ref, *, rb):
    i, e, f = pl.program_id(0), pl.program_id(1), pl.program_id(2)
    n_e, n_f = pl.num_programs(1), pl.num_programs(2)
    tm = xn_ref.shape[0]

    @pl.when((e == 0) & (f == 0))
    def _():
        dest_t_ref[...] = jnp.transpose(dest_ref[...])
        acc_ref[...] = jnp.zeros_like(acc_ref)

    n_blk = (cnt_ref[i * n_e + e] + rb - 1) // rb
    slot_row = dest_t_ref[pl.ds(e, 1), :]
    lane = lax.broadcasted_iota(jnp.int32, (tm, LANES), 1)
    slot_col = jnp.sum(jnp.where(lane == e, dest_ref[...], 0.0), axis=-1, keepdims=True)

    def rows(blk):
        return pl.ds(pl.multiple_of(blk * rb, rb), rb)

    @pl.when(f == 0)
    def _():
        def gather(blk, carry):
            want = (blk * rb + lax.broadcasted_iota(jnp.int32, (rb, tm), 0)).astype(F32)
            pick = jnp.where(slot_row == want, 1.0, 0.0).astype(BF16)
            xc_ref[rows(blk), :] = jnp.dot(pick, xn_ref[...], preferred_element_type=F32).astype(BF16)
            yc_ref[rows(blk), :] = jnp.zeros((rb, D_MODEL), F32)
            return carry
        lax.fori_loop(0, n_blk, gather, 0)

    def expert(blk, carry):
        xc = xc_ref[rows(blk), :]
        hid = _silu(jnp.dot(xc, wg_ref[...], preferred_element_type=F32)) \
            * jnp.dot(xc, wu_ref[...], preferred_element_type=F32)
        yc_ref[rows(blk), :] += jnp.dot(hid.astype(BF16), wd_ref[...], preferred_element_type=F32)
        return carry
    lax.fori_loop(0, n_blk, expert, 0)

    @pl.when(f == n_f - 1)
    def _():
        gate_col = jnp.sum(jnp.where(lane == e, gate_ref[...], 0.0), axis=-1, keepdims=True)

        def scatter(blk, carry):
            want = (blk * rb + lax.broadcasted_iota(jnp.int32, (tm, rb), 1)).astype(F32)
            place = jnp.where(slot_col == want, 1.0, 0.0).astype(BF16)
            acc_ref[...] += gate_col * jnp.dot(place, yc_ref[rows(blk), :].astype(BF16),
                                               preferred_element_type=F32)
            return carry
        lax.fori_loop(0, n_blk, scatter, 0)

    @pl.when((e == n_e - 1) & (f == n_f - 1))
    def _():
        out_ref[...] = _rms(h_ref[...] + acc_ref[...], fg_ref[...])


def _experts(h, xn, gate, dest, w_gate, w_up, w_down, final_g, tm, tf, rb):
    m = h.shape[0]
    n_e, _, d_ff = w_gate.shape
    counts = jnp.sum((dest[:, :n_e] >= 0.0).reshape(m // tm, tm, n_e), axis=1, dtype=jnp.int32).reshape(-1)
    row = lambda i, e, f, c: (i, 0)
    grid_spec = pltpu.PrefetchScalarGridSpec(
        num_scalar_prefetch=1,
        grid=(m // tm, n_e, d_ff // tf),
        in_specs=[pl.BlockSpec((tm, D_MODEL), row), pl.BlockSpec((tm, LANES), row), pl.BlockSpec((tm, LANES), row),
                  pl.BlockSpec((tm, D_MODEL), row),
                  pl.BlockSpec((None, D_MODEL, tf), lambda i, e, f, c: (e, 0, f)),
                  pl.BlockSpec((None, D_MODEL, tf), lambda i, e, f, c: (e, 0, f)),
                  pl.BlockSpec((None, tf, D_MODEL), lambda i, e, f, c: (e, f, 0)),
                  pl.BlockSpec((1, D_MODEL), lambda i, e, f, c: (0, 0))],
        out_specs=pl.BlockSpec((tm, D_MODEL), row),
        scratch_shapes=[pltpu.VMEM((LANES, tm), F32), pltpu.VMEM((tm, D_MODEL), BF16),
                        pltpu.VMEM((tm, D_MODEL), F32), pltpu.VMEM((tm, D_MODEL), F32)],
    )
    return pl.pallas_call(
        functools.partial(_experts_kernel, rb=rb),
        grid_spec=grid_spec,
        out_shape=jax.ShapeDtypeStruct((m, D_MODEL), F32),
        compiler_params=_cparams(3),
        name="experts",
    )(counts, xn, gate, dest, h, w_gate, w_up, w_down, final_g.reshape(1, -1))


def _rope_tables(pos):
    half = HEAD_DIM // 2
    inv = jnp.power(ROPE_THETA, -jnp.arange(half, dtype=F32) / half)
    ang = pos.astype(F32)[:, None] * inv[None, :]
    cos, sin = jnp.cos(ang), jnp.sin(ang)
    cos_t = jnp.concatenate([cos, cos, cos, cos], axis=1)
    sin_t = jnp.concatenate([-sin, sin, -sin, sin], axis=1)
    return cos_t, sin_t


def _rope_apply(y, cos_t, sin_t):
    half = HEAD_DIM // 2
    first = (lax.broadcasted_iota(jnp.int32, (y.shape[0], LANES), 1) % HEAD_DIM) < half
    cols = []
    for c in range(y.shape[1] // LANES):
        yc = y[:, c * LANES:(c + 1) * LANES]
        swapped = jnp.where(first, pltpu.roll(yc, LANES - half, 1), pltpu.roll(yc, half, 1))
        cols.append(yc * cos_t + swapped * sin_t)
    return jnp.concatenate(cols, axis=1)


def _qkv_kernel(x_ref, g_ref, w_ref, cos_ref, sin_ref, q_ref, k_ref, v_ref, wbf_ref, *, head_major):
    @pl.when((pl.program_id(0) == 0) & (pl.program_id(1) == 0))
    def _():
        wbf_ref[...] = w_ref[...].astype(BF16)

    xn = _rms(x_ref[...], g_ref[...]).astype(BF16)
    y = jnp.dot(xn, wbf_ref[...], preferred_element_type=F32)
    cos_t, sin_t = cos_ref[...], sin_ref[...]
    q = _rope_apply(y[:, :D_MODEL], cos_t, sin_t) * (HEAD_DIM ** -0.5)
    k = _rope_apply(y[:, D_MODEL:2 * D_MODEL], cos_t, sin_t)
    v = y[:, 2 * D_MODEL:]
    if head_major:
        tm = q.shape[0]
        for h in range(ATT_HEADS):
            sl = slice(h * HEAD_DIM, (h + 1) * HEAD_DIM)
            q_ref[h] = q[:, sl]
            for p in range(tm // PAGE_SIZE):
                rows = slice(p * PAGE_SIZE, (p + 1) * PAGE_SIZE)
                k_ref[p, h] = k[rows, sl]
                v_ref[p, h] = v[rows, sl]
    else:
        q_ref[...] = q
        k_ref[...] = k
        v_ref[...] = v


def _qkv_rope(x, g, w_qkv, pos, tm, head_major):
    b, t, _ = x.shape
    cos_t, sin_t = _rope_tables(pos)
    fixed = lambda i, j: (0, 0)
    if head_major:
        npg = tm // PAGE_SIZE
        out_specs = [pl.BlockSpec((None, ATT_HEADS, tm, HEAD_DIM), lambda i, j: (i, 0, j, 0)),
                     pl.BlockSpec((None, npg, ATT_HEADS, PAGE_SIZE, HEAD_DIM), lambda i, j: (i, j, 0, 0, 0)),
                     pl.BlockSpec((None, npg, ATT_HEADS, PAGE_SIZE, HEAD_DIM), lambda i, j: (i, j, 0, 0, 0))]
        kv_shape = (b, t // PAGE_SIZE, ATT_HEADS, PAGE_SIZE, HEAD_DIM)
        out_shape = [jax.ShapeDtypeStruct((b, ATT_HEADS, t, HEAD_DIM), F32),
                     jax.ShapeDtypeStruct(kv_shape, F32), jax.ShapeDtypeStruct(kv_shape, F32)]
    else:
        out_specs = [pl.BlockSpec((None, tm, D_MODEL), lambda i, j: (i, j, 0))] * 3
        out_shape = [jax.ShapeDtypeStruct((b, t, D_MODEL), F32)] * 3
    return pl.pallas_call(
        functools.partial(_qkv_kernel, head_major=head_major),
        grid=(b, t // tm),
        in_specs=[pl.BlockSpec((None, tm, D_MODEL), lambda i, j: (i, j, 0)),
                  pl.BlockSpec((1, D_MODEL), fixed),
                  pl.BlockSpec(w_qkv.shape, fixed),
                  pl.BlockSpec((tm, LANES), lambda i, j: (j, 0)),
                  pl.BlockSpec((tm, LANES), lambda i, j: (j, 0))],
        out_specs=out_specs,
        out_shape=out_shape,
        scratch_shapes=[pltpu.VMEM(w_qkv.shape, BF16)],
        compiler_params=_cparams(2),
        name="qkv_rope",
    )(x, g.reshape(1, -1), w_qkv, cos_t, sin_t)


def _top_blocks(gate, valid, n_keep):
    nb = gate.shape[1]
    blk = lax.broadcasted_iota(jnp.int32, gate.shape, 1)
    rank = jnp.zeros(gate.shape, F32)
    for m in range(nb):
        g_m = gate[:, m:m + 1]
        beats = (g_m > gate) | ((g_m == gate) & (m < blk))
        rank = rank + jnp.where(beats & valid[:, m:m + 1], 1.0, 0.0)
    return jnp.where(valid & (rank < n_keep), 1.0, 0.0)


def _moba_prompt_kernel(q_ref, k_ref, v_ref, o_ref, *, t, heads_per_step):
    nb = t // MOBA_BLOCK
    neg = jnp.float32(-jnp.inf)
    outs = []
    for hh in range(heads_per_step):
        q = q_ref[hh]
        k = k_ref[:, hh].reshape(t, HEAD_DIM)
        v = v_ref[:, hh].reshape(t, HEAD_DIM)
        kmean = jnp.concatenate(
            [jnp.mean(k[n * MOBA_BLOCK:(n + 1) * MOBA_BLOCK, :], axis=0, keepdims=True) for n in range(nb)], axis=0)
        gate = lax.dot_general(q, kmean, (((1,), (1,)), ((), ())), precision=HIGHEST,
                               preferred_element_type=F32)
        own_of_q = lax.broadcasted_iota(jnp.int32, (t, nb), 0) // MOBA_BLOCK
        valid = lax.broadcasted_iota(jnp.int32, (t, nb), 1) < own_of_q
        sel = _top_blocks(gate, valid, MOBA_TOPK)
        kb, vb = k.astype(BF16), v.astype(BF16)
        causal = (lax.broadcasted_iota(jnp.int32, (MOBA_BLOCK, MOBA_BLOCK), 1)
                  <= lax.broadcasted_iota(jnp.int32, (MOBA_BLOCK, MOBA_BLOCK), 0))
        o_blocks = []
        for own in range(nb):
            rows = slice(own * MOBA_BLOCK, (own + 1) * MOBA_BLOCK)
            qb = q[rows, :].astype(BF16)
            scores = []
            for n in range(own):
                s = _bdot_t(qb, kb[n * MOBA_BLOCK:(n + 1) * MOBA_BLOCK, :])
                scores.append(jnp.where(sel[rows, n:n + 1] > 0.0, s, neg))
            scores.append(jnp.where(causal, _bdot_t(qb, kb[rows, :]), neg))
            m = scores[0].max(axis=-1, keepdims=True)
            for s in scores[1:]:
                m = jnp.maximum(m, s.max(axis=-1, keepdims=True))
            den = jnp.zeros((MOBA_BLOCK, 1), F32)
            acc = jnp.zeros((MOBA_BLOCK, HEAD_DIM), F32)
            for n, s in enumerate(scores):
                p = jnp.exp(s - m)
                den = den + p.sum(axis=-1, keepdims=True)
                acc = acc + jnp.dot(p.astype(BF16), vb[n * MOBA_BLOCK:(n + 1) * MOBA_BLOCK, :],
                                    preferred_element_type=F32)
            o_blocks.append(acc / den)
        outs.append(jnp.concatenate(o_blocks, axis=0))
    o_ref[...] = jnp.concatenate(outs, axis=1)


def _moba_prompt(q, k_rows, v_rows):
    b, n_h, t, _ = q.shape
    hps = LANES // HEAD_DIM
    npg = t // PAGE_SIZE
    kv_spec = pl.BlockSpec((None, npg, hps, PAGE_SIZE, HEAD_DIM), lambda i, j: (i, 0, j, 0, 0))
    return pl.pallas_call(
        functools.partial(_moba_prompt_kernel, t=t, heads_per_step=hps),
        grid=(b, n_h // hps),
        in_specs=[pl.BlockSpec((None, hps, t, HEAD_DIM), lambda i, j: (i, j, 0, 0)), kv_spec, kv_spec],
        out_specs=pl.BlockSpec((None, t, LANES), lambda i, j: (i, 0, j)),
        out_shape=jax.ShapeDtypeStruct((b, t, n_h * HEAD_DIM), F32),
        compiler_params=_cparams(2),
        name="moba_prompt",
    )(q, k_rows, v_rows)


def _kmean_select_kernel(pt_ref, q_ref, *refs, pages_per_step, n_q, n_blk):
    page_refs = refs[:pages_per_step]
    sel_ref = refs[pages_per_step]
    kmt_ref = refs[pages_per_step + 1]
    s = pl.program_id(1)
    ppb = MOBA_BLOCK // PAGE_SIZE
    rows = ATT_HEADS * HEAD_DIM

    @pl.when(s == 0)
    def _():
        kmt_ref[...] = jnp.zeros_like(kmt_ref)

    lane = lax.broadcasted_iota(jnp.int32, (rows, LANES), 1)
    kmt = kmt_ref[...]
    for blk in range(pages_per_step // ppb):
        tot = page_refs[blk * ppb][...]
        for i in range(1, ppb):
            tot = tot + page_refs[blk * ppb + i][...]
        mean = tot.reshape(rows, PAGE_SIZE).sum(axis=-1, keepdims=True) * (1.0 / MOBA_BLOCK)
        kmt = jnp.where(lane == s * (pages_per_step // ppb) + blk, mean, kmt)
    kmt_ref[...] = kmt

    @pl.when(s == pl.num_programs(1) - 1)
    def _():
        q = q_ref[...]
        n_rows = q.shape[0]
        row_head = lax.broadcasted_iota(jnp.int32, (n_rows, LANES), 0) // n_q
        gate = jnp.zeros((n_rows, LANES), F32)
        for h in range(ATT_HEADS):
            g_h = jnp.dot(q, kmt[h * HEAD_DIM:(h + 1) * HEAD_DIM, :], precision=HIGHEST,
                          preferred_element_type=F32)
            gate = jnp.where(row_head == h, g_h, gate)
        blk_lane = lax.broadcasted_iota(jnp.int32, gate.shape, 1)
        neg = jnp.float32(-jnp.inf)
        gate = jnp.where(blk_lane < n_blk, gate, neg)
        sel = jnp.zeros(gate.shape, jnp.int32)
        for r in range(MOBA_TOPK):
            m = jnp.max(gate, axis=-1, keepdims=True)
            idx = jnp.min(jnp.where(gate == m, blk_lane, LANES), axis=-1, keepdims=True)
            sel = jnp.where(blk_lane == r, idx, sel)
            gate = jnp.where(blk_lane == idx, neg, gate)
        sel_ref[...] = sel


def _sample_select(q, cache_kt, page_table, n_blk, pages_per_step=4):
    db, n_h, n_q, _ = q.shape
    n_pages = page_table.shape[1]
    ppb = MOBA_BLOCK // PAGE_SIZE
    steps = n_blk * ppb // pages_per_step

    def page_spec(i):
        return pl.BlockSpec((None, n_h, HEAD_DIM, PAGE_SIZE),
                            lambda b, s, pt: (pt[b * n_pages + s * pages_per_step + i], 0, 0, 0))

    grid_spec = pltpu.PrefetchScalarGridSpec(
        num_scalar_prefetch=1,
        grid=(db, steps),
        in_specs=[pl.BlockSpec((None, n_h * n_q, HEAD_DIM), lambda b, s, pt: (b, 0, 0))]
                 + [page_spec(i) for i in range(pages_per_step)],
        out_specs=pl.BlockSpec((None, n_h * n_q, LANES), lambda b, s, pt: (b, 0, 0)),
        scratch_shapes=[pltpu.VMEM((n_h * HEAD_DIM, LANES), F32)],
    )
    return pl.pallas_call(
        functools.partial(_kmean_select_kernel, pages_per_step=pages_per_step, n_q=n_q, n_blk=n_blk),
        grid_spec=grid_spec,
        out_shape=jax.ShapeDtypeStruct((db, n_h * n_q, LANES), jnp.int32),
        compiler_params=_cparams(2),
        name="sample_select",
    )(page_table.reshape(-1), q.reshape(db, n_h * n_q, HEAD_DIM), *([cache_kt] * pages_per_step))


def _sample_attn_kernel(sel_ref, pt_ref, q_ref, kn_ref, vn_ref, *refs, n_q, tiles_per_q):
    n_tiles = n_q * tiles_per_q
    k_refs = refs[:n_tiles]
    v_refs = refs[n_tiles:2 * n_tiles]
    o_ref = refs[2 * n_tiles]
    neg = jnp.float32(-jnp.inf)
    q = q_ref[...]
    kn, vn = kn_ref[...], vn_ref[...]
    rows8 = 8
    q8 = jnp.concatenate([q] * (rows8 // n_q), axis=0)
    kt_cat = jnp.concatenate([r[...] for r in k_refs], axis=1)
    vt_cat = jnp.concatenate([r[...] for r in v_refs], axis=1)
    s = _bdot(q8, kt_cat)
    row = lax.broadcasted_iota(jnp.int32, s.shape, 0) % n_q
    col = lax.broadcasted_iota(jnp.int32, s.shape, 1)
    s = jnp.where((col // (tiles_per_q * PAGE_SIZE)) == row, s, neg)
    row_o = lax.broadcasted_iota(jnp.int32, (rows8, 1), 0) % n_q
    s_own = [jnp.where(row_o >= c, jnp.sum(q8 * kn[c:c + 1, :], axis=-1, keepdims=True), neg)
             for c in range(n_q)]
    m = s.max(axis=-1, keepdims=True)
    for s_c in s_own:
        m = jnp.maximum(m, s_c)
    p = jnp.exp(s - m)
    den = p.sum(axis=-1, keepdims=True)
    acc = _bdot_t(p, vt_cat)
    for c, s_c in enumerate(s_own):
        p_c = jnp.exp(s_c - m)
        den = den + p_c
        acc = acc + p_c * vn[c:c + 1, :]
    o_ref[...] = (acc / den)[:n_q, :]


def _sample_attn(q, k_new, v_new, cache_kt, cache_vt, sel, page_table):
    db, n_h, n_q, _ = q.shape
    ppb = MOBA_BLOCK // PAGE_SIZE
    tiles_per_q = MOBA_TOPK * ppb
    n_tiles = n_q * tiles_per_q
    n_pages = page_table.shape[1]

    def tile_spec(i):
        query, r, pg = i // tiles_per_q, (i % tiles_per_q) // ppb, i % ppb

        def index_map(b, h, sel_ref, pt_ref):
            blk = sel_ref[((b * n_h + h) * n_q + query) * MOBA_TOPK + r]
            return (pt_ref[b * n_pages + blk * ppb + pg], h, 0, 0)

        return pl.BlockSpec((None, None, HEAD_DIM, PAGE_SIZE), index_map)

    small = pl.BlockSpec((None, None, n_q, HEAD_DIM), lambda b, h, sel_ref, pt_ref: (b, h, 0, 0))
    grid_spec = pltpu.PrefetchScalarGridSpec(
        num_scalar_prefetch=2,
        grid=(db, n_h),
        in_specs=[small, small, small] + [tile_spec(i) for i in range(n_tiles)] * 2,
        out_specs=small,
    )
    return pl.pallas_call(
        functools.partial(_sample_attn_kernel, n_q=n_q, tiles_per_q=tiles_per_q),
        grid_spec=grid_spec,
        out_shape=jax.ShapeDtypeStruct((db, n_h, n_q, HEAD_DIM), F32),
        compiler_params=_cparams(2),
        name="sample_attn",
    )(sel, page_table.reshape(-1), q, k_new, v_new, *([cache_kt] * n_tiles), *([cache_vt] * n_tiles))


def _even_layer(x, conv_prev, gla_prev, p, tm, conv_tt, gla_c, ffn_tm, ffn_tf):
    b, t, d = x.shape
    xf = x.reshape(b * t, d)
    glu, q, k, v, gate, la = _inproj(xf, p["norm_mix"], p["w_in"], p["w_a2"], p["b_a"], tm)
    as_bt = lambda a: a.reshape(b, t, a.shape[-1])
    conv_out, conv_state = _conv_module(as_bt(glu), conv_prev, p["conv_w"], p["conv_b"], p["ln_g"], p["ln_b"], conv_tt)
    o, gla_state = _gla(as_bt(q), as_bt(k), as_bt(la), as_bt(v), as_bt(gate),
                        gla_prev.reshape(b, GLA_HEADS * GLA_DK, GLA_DV), p["gla_norm_g"], gla_c)
    h = _outproj(xf, [conv_out.reshape(b * t, -1), o.reshape(b * t, -1)], p["w_out"], tm)
    h = _ffn(h, p["norm_ffn"], p["ffn_w_gate"], p["ffn_w_up"], p["ffn_w_down"], ffn_tm, ffn_tf)
    return h.reshape(b, t, d), conv_state, gla_state.reshape(b, GLA_HEADS, GLA_DK, GLA_DV)


def kernel(x_prompt, x_sample, state_conv, state_gla, cache_k, cache_v, page_table, norm_mix_even, w_in_even, conv_w, conv_b, conv_ln_g, conv_ln_b, gla_w_a2, gla_b_a, gla_norm_g, w_out_even, norm_ffn_even, ffn_w_gate, ffn_w_up, ffn_w_down, norm_mix_odd, w_qkv_odd, w_out_odd, norm_ffn_odd, router_w, moe_w_gate, moe_w_up, moe_w_down, final_norm_g):
    bp, tp, d = x_prompt.shape
    db, ts, _ = x_sample.shape
    even = dict(norm_mix=norm_mix_even[0], w_in=w_in_even[0], conv_w=conv_w[0], conv_b=conv_b[0],
                ln_g=conv_ln_g[0], ln_b=conv_ln_b[0], w_a2=gla_w_a2[0], b_a=gla_b_a[0],
                gla_norm_g=gla_norm_g[0], w_out=w_out_even[0], norm_ffn=norm_ffn_even[0],
                ffn_w_gate=ffn_w_gate, ffn_w_up=ffn_w_up, ffn_w_down=ffn_w_down)

    hp, conv_p, gla_p = _even_layer(
        x_prompt, jnp.zeros((bp, CONV_W - 1, D_CONV), F32), jnp.zeros((bp, GLA_HEADS, GLA_DK, GLA_DV), F32),
        even, tm=512, conv_tt=256, gla_c=GLA_CHUNK, ffn_tm=1024, ffn_tf=256)

    m_s = db * ts
    xs = x_sample.reshape(m_s, d)
    glu, q, k, v, gate, la = _inproj(xs, even["norm_mix"], even["w_in"], even["w_a2"], even["b_a"], m_s)
    as_bt = lambda a: a.reshape(db, ts, a.shape[-1])
    conv_out_s, conv_s = _conv_module(as_bt(glu), state_conv[0], even["conv_w"], even["conv_b"],
                                      even["ln_g"], even["ln_b"], ts)
    pad_t = 16
    pad8 = lambda a: jnp.pad(as_bt(a), ((0, 0), (0, pad_t - ts), (0, 0)))
    o_s, gla_s = _gla(pad8(q), pad8(k), pad8(la), pad8(v), pad8(gate),
                      state_gla[0].reshape(db, GLA_HEADS * GLA_DK, GLA_DV), even["gla_norm_g"], pad_t)
    hs = _outproj(xs, [conv_out_s.reshape(m_s, -1), o_s[:, :ts].reshape(m_s, -1)], even["w_out"], m_s)
    hs = _ffn(hs, even["norm_ffn"], ffn_w_gate, ffn_w_up, ffn_w_down, m_s, 256)
    gla_s = gla_s.reshape(db, GLA_HEADS, GLA_DK, GLA_DV)

    qp, k_rows, v_rows = _qkv_rope(hp, norm_mix_odd[0], w_qkv_odd[0], jnp.arange(tp), 256, True)
    op = _moba_prompt(qp, k_rows, v_rows)
    moe_wg, moe_wu, moe_wd = moe_w_gate[0].astype(BF16), moe_w_up[0].astype(BF16), moe_w_down[0].astype(BF16)
    hp2, xn_p, gate_p, dest_p = _route(hp.reshape(bp * tp, d), op.reshape(bp * tp, d), w_out_odd[0],
                                       norm_ffn_odd[0], router_w[0], 1024)
    y_prompt = _experts(hp2, xn_p, gate_p, dest_p, moe_wg, moe_wu, moe_wd, final_norm_g,
                        tm=1024, tf=896, rb=128).reshape(bp, tp, d)

    n_blocks = page_table.shape[1] * PAGE_SIZE // MOBA_BLOCK
    past_len = page_table.shape[1] * PAGE_SIZE
    pos_s = past_len + (jnp.arange(m_s) % ts)
    qs, ks, vs = _qkv_rope(hs.reshape(1, m_s, d), norm_mix_odd[0], w_qkv_odd[0], pos_s, m_s, False)
    to_heads = lambda a: a.reshape(db, ts, ATT_HEADS, HEAD_DIM).transpose(0, 2, 1, 3)
    qs, ks, vs = to_heads(qs), to_heads(ks), to_heads(vs)
    cache_kt, cache_vt = jnp.swapaxes(cache_k[0], -1, -2), jnp.swapaxes(cache_v[0], -1, -2)
    sel = _sample_select(qs, cache_kt, page_table, n_blocks)[:, :, :MOBA_TOPK].reshape(-1)
    os_ = _sample_attn(qs, ks, vs, cache_kt, cache_vt, sel, page_table)
    os_ = os_.transpose(0, 2, 1, 3).reshape(m_s, d)
    hs2 = _outproj(hs, [os_], w_out_odd[0], m_s)
    y_sample = _ffn(hs2, norm_ffn_odd[0], moe_wg, moe_wu, moe_wd, m_s, 512,
                    router_w=router_w[0], final_g=final_norm_g).reshape(db, ts, d)

    return (y_prompt, y_sample, conv_p[None], gla_p[None], k_rows[None], v_rows[None],
            conv_s[None], gla_s[None], ks[None], vs[None])
```

```python
import functools

import jax
import jax.numpy as jnp
import numpy as np
from jax import lax
from jax.experimental import pallas as pl
from jax.experimental.pallas import tpu as pltpu

F32 = jnp.float32
BF16 = jnp.bfloat16
HIGHEST = lax.Precision.HIGHEST

LANES = 128
NORM_EPS = 1e-6
D_MODEL = 1024
D_CONV = 512
CONV_W = 31
CONV_HALO = 32
GLA_HEADS = 4
GLA_DK = 64
GLA_DV = 128
GLA_TAU = 16.0
GLA_CHUNK = 64
ATT_HEADS = 16
HEAD_DIM = 64
MOBA_BLOCK = 256
MOBA_TOPK = 3
PAGE_SIZE = 128
ROPE_THETA = 10000.0
N_EXPERTS = 8
VMEM_LIMIT = 60 * 1024 * 1024


def _cparams(n_axes):
    return pltpu.CompilerParams(dimension_semantics=("arbitrary",) * n_axes,
                                vmem_limit_bytes=VMEM_LIMIT)


def _rms(x, g):
    return x * lax.rsqrt(jnp.mean(x * x, axis=-1, keepdims=True) + NORM_EPS) * g


def _silu(x):
    return x * jax.nn.sigmoid(x)


def _bdot(a, b):
    return jnp.dot(a.astype(BF16), b.astype(BF16), preferred_element_type=F32)


def _bdot_t(a, b):
    return lax.dot_general(a.astype(BF16), b.astype(BF16), (((1,), (1,)), ((), ())),
                           preferred_element_type=F32)


def _tdot(a, b):
    return lax.dot_general(a.astype(BF16), b.astype(BF16), (((0,), (0,)), ((), ())),
                           preferred_element_type=F32)


def _inproj_kernel(x_ref, g_ref, w_ref, wa1_ref, wa2_ref, ba_ref,
                   glu_ref, q_ref, k_ref, v_ref, gate_ref, la_ref, wbf_ref, wa1bf_ref):
    @pl.when(pl.program_id(0) == 0)
    def _():
        wbf_ref[...] = w_ref[...].astype(BF16)
        wa1bf_ref[...] = wa1_ref[...].astype(BF16)

    xn = _rms(x_ref[...], g_ref[...]).astype(BF16)
    y = jnp.dot(xn, wbf_ref[...], preferred_element_type=F32)
    glu_ref[...] = y[:, :D_CONV] * jax.nn.sigmoid(y[:, D_CONV:2 * D_CONV])
    q_ref[...] = y[:, 1024:1280] * (GLA_DK ** -0.5)
    k_ref[...] = y[:, 1280:1536]
    v_ref[...] = y[:, 1536:2048]
    gate_ref[...] = y[:, 2048:2560]
    a = jnp.dot(xn, wa1bf_ref[...], preferred_element_type=F32)
    z = jnp.dot(a, wa2_ref[...], precision=HIGHEST, preferred_element_type=F32) + ba_ref[...]
    log_sig = jnp.minimum(z, 0.0) - jnp.log(1.0 + jnp.exp(-jnp.abs(z)))
    la_ref[...] = log_sig * (1.0 / GLA_TAU)


def _inproj(x, g, w_in, w_a2, b_a, tm):
    m = x.shape[0]
    n_main = 2560
    wa1 = jnp.pad(w_in[:, n_main:], ((0, 0), (0, LANES - (w_in.shape[1] - n_main))))
    wa2 = jnp.pad(w_a2, ((0, LANES - w_a2.shape[0]), (0, 0)))
    row = lambda i: (i, 0)
    fixed = lambda i: (0, 0)
    outs = [(D_CONV, F32), (256, F32), (256, F32), (512, F32), (512, F32), (256, F32)]
    return pl.pallas_call(
        _inproj_kernel,
        grid=(m // tm,),
        in_specs=[pl.BlockSpec((tm, D_MODEL), row),
                  pl.BlockSpec((1, D_MODEL), fixed),
                  pl.BlockSpec((D_MODEL, n_main), fixed),
                  pl.BlockSpec((D_MODEL, LANES), fixed),
                  pl.BlockSpec((LANES, 256), fixed),
                  pl.BlockSpec((1, 256), fixed)],
        out_specs=[pl.BlockSpec((tm, n), row) for n, _ in outs],
        out_shape=[jax.ShapeDtypeStruct((m, n), dt) for n, dt in outs],
        scratch_shapes=[pltpu.VMEM((D_MODEL, n_main), BF16), pltpu.VMEM((D_MODEL, LANES), BF16)],
        compiler_params=_cparams(1),
        name="inproj",
    )(x, g.reshape(1, -1), w_in, wa1, wa2, b_a.reshape(1, -1))


def _conv_kernel(glu_ref, prev_ref, w_ref, b_ref, lg_ref, lb_ref, out_ref, st_ref, full_ref, *, tt, rc):
    t = pl.program_id(1)
    lo = CONV_HALO - (CONV_W - 1)

    @pl.when(t == 0)
    def _():
        full_ref[0:lo, :] = jnp.zeros((lo, D_CONV), F32)
        full_ref[lo:CONV_HALO, :] = prev_ref[...]

    full_ref[CONV_HALO:CONV_HALO + tt, :] = glu_ref[...]
    for c in range(tt // rc):
        base = c * rc + lo
        acc = full_ref[base:base + rc, :] * w_ref[0:1, :]
        for j in range(1, CONV_W):
            acc = acc + full_ref[base + j:base + j + rc, :] * w_ref[j:j + 1, :]
        y = acc + b_ref[...]
        mu = jnp.mean(y, axis=-1, keepdims=True)
        d = y - mu
        var = jnp.mean(d * d, axis=-1, keepdims=True)
        yn = d * lax.rsqrt(var + NORM_EPS) * lg_ref[...] + lb_ref[...]
        out_ref[c * rc:(c + 1) * rc, :] = _silu(yn).astype(out_ref.dtype)

    tail = full_ref[tt + lo:tt + CONV_HALO, :]

    @pl.when(t == pl.num_programs(1) - 1)
    def _():
        st_ref[...] = tail

    full_ref[lo:CONV_HALO, :] = tail


def _conv_module(glu, prev, conv_w, conv_b, ln_g, ln_b, tt):
    b, t, _ = glu.shape
    rc = min(tt, 32)
    fixed = lambda i, j: (0, 0)
    return pl.pallas_call(
        functools.partial(_conv_kernel, tt=tt, rc=rc),
        grid=(b, t // tt),
        in_specs=[pl.BlockSpec((None, tt, D_CONV), lambda i, j: (i, j, 0)),
                  pl.BlockSpec((None, CONV_W - 1, D_CONV), lambda i, j: (i, 0, 0)),
                  pl.BlockSpec((CONV_W, D_CONV), fixed),
                  pl.BlockSpec((1, D_CONV), fixed),
                  pl.BlockSpec((1, D_CONV), fixed),
                  pl.BlockSpec((1, D_CONV), fixed)],
        out_specs=[pl.BlockSpec((None, tt, D_CONV), lambda i, j: (i, j, 0)),
                   pl.BlockSpec((None, CONV_W - 1, D_CONV), lambda i, j: (i, 0, 0))],
        out_shape=[jax.ShapeDtypeStruct((b, t, D_CONV), BF16),
                   jax.ShapeDtypeStruct((b, CONV_W - 1, D_CONV), F32)],
        scratch_shapes=[pltpu.VMEM((CONV_HALO + tt, D_CONV), F32)],
        compiler_params=_cparams(2),
        name="conv_module",
    )(glu, prev, conv_w, conv_b.reshape(1, -1), ln_g.reshape(1, -1), ln_b.reshape(1, -1))


def _gla_levels(c):
    levels, g_big = [], c
    while g_big > 1:
        g_small = max(g_big // 4, 1)
        levels.append((g_big, g_small))
        g_big = g_small
    return levels


def _gla_decay_matrices(c):
    t = np.arange(c)[:, None]
    s = np.arange(c)[None, :]
    mats = [(s <= t), (s > t)]
    for g_big, g_small in _gla_levels(c):
        sub = (t % g_big) // g_small
        ref_q = (t // g_big) * g_big + sub * g_small - 1
        mats.append((sub >= 1) & (s > ref_q) & (s <= t))
        for i in range(1, g_big // g_small):
            ref_k = (t // g_big) * g_big + i * g_small - 1
            mats.append(((t % g_big) < i * g_small) & (s > t) & (s <= ref_k))
    return np.concatenate([m.astype(np.float32) for m in mats], axis=0)


def _gla_kernel(q_ref, k_ref, la_ref, v_ref, gate_ref, s0_ref, ng_ref, dmat_ref,
                out_ref, st_ref, s_ref, *, c):
    ci = pl.program_id(1)

    @pl.when(ci == 0)
    def _():
        s_ref[...] = s0_ref[...]

    for bi in range(q_ref.shape[0]):
        out, new_state = _gla_chunk(q_ref[bi], k_ref[bi], la_ref[bi], v_ref[bi], gate_ref[bi], s_ref[bi],
                                    ng_ref[...], dmat_ref[...], c)
        out_ref[bi] = out.astype(out_ref.dtype)
        s_ref[bi] = new_state

    @pl.when(ci == pl.num_programs(1) - 1)
    def _():
        st_ref[...] = s_ref[...]


def _gla_chunk(q, k, la, v, gate, state, norm_g, dmat, c):
    hk = GLA_HEADS * GLA_DK
    la1 = la.astype(BF16)
    r1 = la - la1.astype(F32)
    la2 = r1.astype(BF16)
    la3 = (r1 - la2.astype(F32)).astype(BF16)
    e_all = (jnp.dot(dmat, la1, preferred_element_type=F32)
             + jnp.dot(dmat, la2, preferred_element_type=F32)
             + jnp.dot(dmat, la3, preferred_element_type=F32))

    def e_blk(i):
        return e_all[i * c:(i + 1) * c, :]

    lane_head = lax.broadcasted_iota(jnp.int32, (c, hk), 1) // GLA_DK
    head_masks = [(lane_head == h).astype(F32) for h in range(GLA_HEADS)]

    def stack_heads(x):
        return jnp.concatenate([x * m for m in head_masks], axis=0).astype(BF16)

    row_t = lax.broadcasted_iota(jnp.int32, (GLA_HEADS * c, c), 0) % c
    col_s = lax.broadcasted_iota(jnp.int32, (GLA_HEADS * c, c), 1)
    row_k = lax.broadcasted_iota(jnp.int32, (c, hk), 0)

    o_st = jnp.dot(stack_heads(q * jnp.exp(e_blk(0))), state.astype(BF16),
                   preferred_element_type=F32)

    att = jnp.where(row_t == col_s, _bdot_t(stack_heads(q), k), 0.0)
    idx = 2
    for g_big, g_small in _gla_levels(c):
        q_lvl = stack_heads(q * jnp.exp(e_blk(idx)))
        idx += 1
        same_blk = (row_t // g_big) == (col_s // g_big)
        sub_t = (row_t % g_big) // g_small
        for i in range(1, g_big // g_small):
            k_i = jnp.where((row_k % g_big) < i * g_small, k * jnp.exp(e_blk(idx)), 0.0)
            idx += 1
            p = _bdot_t(q_lvl, k_i)
            att = att + jnp.where(same_blk & (sub_t == i), p, 0.0)

    att = att.astype(BF16)
    outs = []
    for h in range(GLA_HEADS):
        v_h = v[:, h * GLA_DV:(h + 1) * GLA_DV]
        o_h = o_st[h * c:(h + 1) * c, :] + jnp.dot(att[h * c:(h + 1) * c, :], v_h.astype(BF16),
                                                   preferred_element_type=F32)
        o_h = _rms(o_h, norm_g[:, h * GLA_DV:(h + 1) * GLA_DV])
        outs.append(o_h * _silu(gate[:, h * GLA_DV:(h + 1) * GLA_DV]))

    kd = (k * jnp.exp(e_blk(1))).astype(BF16)
    ones = jnp.ones((c, GLA_DV), BF16)
    b_last = _tdot(la1, ones) + _tdot(la2, ones) + _tdot(la3, ones)
    row_head = lax.broadcasted_iota(jnp.int32, (hk, GLA_DV), 0) // GLA_DK
    upd = jnp.zeros((hk, GLA_DV), F32)
    for h in range(GLA_HEADS):
        kv = _tdot(kd, v[:, h * GLA_DV:(h + 1) * GLA_DV])
        upd = upd + jnp.where(row_head == h, kv, 0.0)
    return jnp.concatenate(outs, axis=1), jnp.exp(b_last) * state + upd


def _gla(q, k, la, v, gate, s0, norm_g, c, nbs):
    b, t, hk = q.shape
    dv_all = v.shape[-1]
    dmat = jnp.asarray(_gla_decay_matrices(c), dtype=BF16)
    tok = lambda n: pl.BlockSpec((nbs, c, n), lambda i, j: (i, j, 0))
    fixed = lambda i, j: (0, 0)
    return pl.pallas_call(
        functools.partial(_gla_kernel, c=c),
        grid=(b // nbs, t // c),
        in_specs=[tok(hk), tok(hk), tok(hk), tok(dv_all), tok(dv_all),
                  pl.BlockSpec((nbs, hk, GLA_DV), lambda i, j: (i, 0, 0)),
                  pl.BlockSpec((1, dv_all), fixed),
                  pl.BlockSpec(dmat.shape, fixed)],
        out_specs=[tok(dv_all), pl.BlockSpec((nbs, hk, GLA_DV), lambda i, j: (i, 0, 0))],
        out_shape=[jax.ShapeDtypeStruct((b, t, dv_all), BF16),
                   jax.ShapeDtypeStruct((b, hk, GLA_DV), F32)],
        scratch_shapes=[pltpu.VMEM((nbs, hk, GLA_DV), F32)],
        compiler_params=_cparams(2),
        name="gla",
    )(q, k, la, v, gate, s0, norm_g.reshape(1, -1), dmat)


def _outproj_kernel(*refs, n_parts):
    x_ref = refs[0]
    a_refs = refs[1:1 + n_parts]
    w_ref = refs[1 + n_parts]
    out_ref = refs[2 + n_parts]
    wbf_ref = refs[3 + n_parts]

    @pl.when(pl.program_id(0) == 0)
    def _():
        wbf_ref[...] = w_ref[...].astype(BF16)

    acc = x_ref[...]
    off = 0
    for a_ref in a_refs:
        n = a_ref.shape[-1]
        acc = acc + jnp.dot(a_ref[...].astype(BF16), wbf_ref[off:off + n, :], preferred_element_type=F32)
        off += n
    out_ref[...] = acc


def _outproj(x, parts, w, tm):
    m = x.shape[0]
    row = lambda i: (i, 0)
    return pl.pallas_call(
        functools.partial(_outproj_kernel, n_parts=len(parts)),
        grid=(m // tm,),
        in_specs=[pl.BlockSpec((tm, D_MODEL), row)]
                 + [pl.BlockSpec((tm, p.shape[-1]), row) for p in parts]
                 + [pl.BlockSpec(w.shape, lambda i: (0, 0))],
        out_specs=pl.BlockSpec((tm, D_MODEL), row),
        out_shape=jax.ShapeDtypeStruct((m, D_MODEL), F32),
        scratch_shapes=[pltpu.VMEM(w.shape, BF16)],
        compiler_params=_cparams(1),
        name="outproj",
    )(x, *parts, w)


def _ffn_kernel(*refs, routed, final_norm):
    it = iter(refs)
    h_ref, g_ref = next(it), next(it)
    rw_ref = next(it) if routed else None
    wg_ref, wu_ref, wd_ref = next(it), next(it), next(it)
    fg_ref = next(it) if final_norm else None
    out_ref, xn_ref, acc_ref = next(it), next(it), next(it)
    gate_ref = next(it) if routed else None
    e, f = pl.program_id(1), pl.program_id(2)

    @pl.when((e == 0) & (f == 0))
    def _():
        xn = _rms(h_ref[...], g_ref[...])
        xn_ref[...] = xn.astype(BF16)
        acc_ref[...] = jnp.zeros_like(acc_ref)
        if routed:
            gate_ref[...] = _top2_gates(jnp.dot(xn, rw_ref[...], precision=HIGHEST, preferred_element_type=F32))

    xn = xn_ref[...]
    hid = _silu(jnp.dot(xn, wg_ref[...].astype(BF16), preferred_element_type=F32)) \
        * jnp.dot(xn, wu_ref[...].astype(BF16), preferred_element_type=F32)
    if routed:
        lane = lax.broadcasted_iota(jnp.int32, gate_ref.shape, 1)
        hid = hid * jnp.sum(jnp.where(lane == e, gate_ref[...], 0.0), axis=-1, keepdims=True)
    acc_ref[...] += jnp.dot(hid.astype(BF16), wd_ref[...].astype(BF16), preferred_element_type=F32)

    @pl.when((e == pl.num_programs(1) - 1) & (f == pl.num_programs(2) - 1))
    def _():
        y = h_ref[...] + acc_ref[...]
        if final_norm:
            y = _rms(y, fg_ref[...])
        out_ref[...] = y


def _ffn(h, g, w_gate, w_up, w_down, tm, tf, router_w=None, final_g=None):
    m = h.shape[0]
    n_e, _, d_ff = w_gate.shape
    routed = router_w is not None
    final_norm = final_g is not None
    row = lambda i, e, f: (i, 0)
    fixed = lambda i, e, f: (0, 0)
    args = [h, g.reshape(1, -1)]
    in_specs = [pl.BlockSpec((tm, D_MODEL), row), pl.BlockSpec((1, D_MODEL), fixed)]
    if routed:
        args.append(jnp.pad(router_w, ((0, 0), (0, LANES - router_w.shape[1]))))
        in_specs.append(pl.BlockSpec((D_MODEL, LANES), fixed))
    args += [w_gate, w_up, w_down]
    in_specs += [pl.BlockSpec((None, D_MODEL, tf), lambda i, e, f: (e, 0, f)),
                 pl.BlockSpec((None, D_MODEL, tf), lambda i, e, f: (e, 0, f)),
                 pl.BlockSpec((None, tf, D_MODEL), lambda i, e, f: (e, f, 0))]
    if final_norm:
        args.append(final_g.reshape(1, -1))
        in_specs.append(pl.BlockSpec((1, D_MODEL), fixed))
    scratch = [pltpu.VMEM((tm, D_MODEL), BF16), pltpu.VMEM((tm, D_MODEL), F32)]
    if routed:
        scratch.append(pltpu.VMEM((tm, LANES), F32))
    return pl.pallas_call(
        functools.partial(_ffn_kernel, routed=routed, final_norm=final_norm),
        grid=(m // tm, n_e, d_ff // tf),
        in_specs=in_specs,
        out_specs=pl.BlockSpec((tm, D_MODEL), row),
        out_shape=jax.ShapeDtypeStruct((m, D_MODEL), F32),
        scratch_shapes=scratch,
        compiler_params=_cparams(3),
        name="moe_ffn" if routed else "ffn",
    )(*args)


def _top2_gates(logits):
    lane = lax.broadcasted_iota(jnp.int32, logits.shape, 1)
    neg = jnp.float32(-jnp.inf)
    logits = jnp.where(lane < N_EXPERTS, logits, neg)
    m1 = jnp.max(logits, axis=-1, keepdims=True)
    i1 = jnp.min(jnp.where(logits == m1, lane, LANES), axis=-1, keepdims=True)
    rest = jnp.where(lane == i1, neg, logits)
    m2 = jnp.max(rest, axis=-1, keepdims=True)
    i2 = jnp.min(jnp.where(rest == m2, lane, LANES), axis=-1, keepdims=True)
    e2 = jnp.exp(m2 - m1)
    g1 = 1.0 / (1.0 + e2)
    g2 = e2 / (1.0 + e2)
    return jnp.where(lane == i1, g1, 0.0) + jnp.where(lane == i2, g2, 0.0)


def _route_kernel(x_ref, a_ref, w_ref, g_ref, rw_ref, h_ref, xn_ref, gate_ref, dest_ref, wbf_ref, ltri_ref):
    tm = x_ref.shape[0]

    @pl.when(pl.program_id(0) == 0)
    def _():
        wbf_ref[...] = w_ref[...].astype(BF16)
        earlier = (lax.broadcasted_iota(jnp.int32, (tm, tm), 1) < lax.broadcasted_iota(jnp.int32, (tm, tm), 0))
        ltri_ref[...] = jnp.where(earlier, 1.0, 0.0).astype(BF16)

    h = x_ref[...] + jnp.dot(a_ref[...].astype(BF16), wbf_ref[...], preferred_element_type=F32)
    h_ref[...] = h
    xn = _rms(h, g_ref[...])
    xn_ref[...] = xn.astype(BF16)
    gate = _top2_gates(jnp.dot(xn, rw_ref[...], precision=HIGHEST, preferred_element_type=F32))
    gate_ref[...] = gate
    routed = gate > 0.0
    rank = jnp.dot(ltri_ref[...], jnp.where(routed, 1.0, 0.0).astype(BF16), preferred_element_type=F32)
    dest_ref[...] = jnp.where(routed, rank, -1.0)


def _route(x, a, w_out, g, router_w, tm):
    m = x.shape[0]
    row = lambda i: (i, 0)
    fixed = lambda i: (0, 0)
    rw = jnp.pad(router_w, ((0, 0), (0, LANES - router_w.shape[1])))
    return pl.pallas_call(
        _route_kernel,
        grid=(m // tm,),
        in_specs=[pl.BlockSpec((tm, D_MODEL), row), pl.BlockSpec((tm, D_MODEL), row),
                  pl.BlockSpec(w_out.shape, fixed), pl.BlockSpec((1, D_MODEL), fixed),
                  pl.BlockSpec((D_MODEL, LANES), fixed)],
        out_specs=[pl.BlockSpec((tm, D_MODEL), row), pl.BlockSpec((tm, D_MODEL), row),
                   pl.BlockSpec((tm, LANES), row), pl.BlockSpec((tm, LANES), row)],
        out_shape=[jax.ShapeDtypeStruct((m, D_MODEL), F32), jax.ShapeDtypeStruct((m, D_MODEL), BF16),
                   jax.ShapeDtypeStruct((m, LANES), F32), jax.ShapeDtypeStruct((m, LANES), F32)],
        scratch_shapes=[pltpu.VMEM(w_out.shape, BF16), pltpu.VMEM((tm, tm), BF16)],
        compiler_params=_cparams(1),
        name="route",
    )(x, a, w_out, g.reshape(1, -1), rw)


def _experts_kernel(cnt_ref, xn_ref, gate_ref, dest_ref, h_ref, wg_ref, wu_ref, wd_ref, fg_ref, out_ref,
                    dest_t_ref, xc_ref, yc_ref, *, rb):
    i, e, f = pl.program_id(0), pl.program_id(1), pl.program_id(2)
    n_e, n_f = pl.num_programs(1), pl.num_programs(2)
    tm = xn_ref.shape[0]

    @pl.when((e == 0) & (f == 0))
    def _():
        dest_t_ref[...] = jnp.transpose(dest_ref[...])
        out_ref[...] = h_ref[...]

    n_blk = (cnt_ref[i * n_e + e] + rb - 1) // rb
    slot_row = dest_t_ref[pl.ds(e, 1), :]
    lane = lax.broadcasted_iota(jnp.int32, (tm, LANES), 1)
    slot_col = jnp.sum(jnp.where(lane == e, dest_ref[...], 0.0), axis=-1, keepdims=True)

    def rows(blk):
        return pl.ds(pl.multiple_of(blk * rb, rb), rb)

    @pl.when(f == 0)
    def _():
        def gather(blk, carry):
            want = (blk * rb + lax.broadcasted_iota(jnp.int32, (rb, tm), 0)).astype(F32)
            pick = jnp.where(slot_row == want, 1.0, 0.0).astype(BF16)
            xc_ref[rows(blk), :] = jnp.dot(pick, xn_ref[...], preferred_element_type=F32).astype(BF16)
            yc_ref[rows(blk), :] = jnp.zeros((rb, D_MODEL), F32)
            return carry
        lax.fori_loop(0, n_blk, gather, 0)

    def expert(blk, carry):
        xc = xc_ref[rows(blk), :]
        hid = _silu(jnp.dot(xc, wg_ref[...], preferred_element_type=F32)) \
            * jnp.dot(xc, wu_ref[...], preferred_element_type=F32)
        yc_ref[rows(blk), :] += jnp.dot(hid.astype(BF16), wd_ref[...], preferred_element_type=F32)
        return carry
    lax.fori_loop(0, n_blk, expert, 0)

    @pl.when(f == n_f - 1)
    def _():
        gate_col = jnp.sum(jnp.where(lane == e, gate_ref[...], 0.0), axis=-1, keepdims=True)

        def scatter(blk, carry):
            want = (blk * rb + lax.broadcasted_iota(jnp.int32, (tm, rb), 1)).astype(F32)
            place = jnp.where(slot_col == want, 1.0, 0.0).astype(BF16)
            out_ref[...] += gate_col * jnp.dot(place, yc_ref[rows(blk), :].astype(BF16),
                                               preferred_element_type=F32)
            return carry
        lax.fori_loop(0, n_blk, scatter, 0)

    @pl.when((e == n_e - 1) & (f == n_f - 1))
    def _():
        out_ref[...] = _rms(out_ref[...], fg_ref[...])


def _experts(h, xn, gate, dest, w_gate, w_up, w_down, final_g, tm, tf, rb):
    m = h.shape[0]
    n_e, _, d_ff = w_gate.shape
    counts = jnp.sum((dest[:, :n_e] >= 0.0).reshape(m // tm, tm, n_e), axis=1, dtype=jnp.int32).reshape(-1)
    row = lambda i, e, f, c: (i, 0)
    grid_spec = pltpu.PrefetchScalarGridSpec(
        num_scalar_prefetch=1,
        grid=(m // tm, n_e, d_ff // tf),
        in_specs=[pl.BlockSpec((tm, D_MODEL), row), pl.BlockSpec((tm, LANES), row), pl.BlockSpec((tm, LANES), row),
                  pl.BlockSpec((tm, D_MODEL), row),
                  pl.BlockSpec((None, D_MODEL, tf), lambda i, e, f, c: (e, 0, f)),
                  pl.BlockSpec((None, D_MODEL, tf), lambda i, e, f, c: (e, 0, f)),
                  pl.BlockSpec((None, tf, D_MODEL), lambda i, e, f, c: (e, f, 0)),
                  pl.BlockSpec((1, D_MODEL), lambda i, e, f, c: (0, 0))],
        out_specs=pl.BlockSpec((tm, D_MODEL), row),
        scratch_shapes=[pltpu.VMEM((LANES, tm), F32), pltpu.VMEM((tm, D_MODEL), BF16),
                        pltpu.VMEM((tm, D_MODEL), F32)],
    )
    return pl.pallas_call(
        functools.partial(_experts_kernel, rb=rb),
        grid_spec=grid_spec,
        out_shape=jax.ShapeDtypeStruct((m, D_MODEL), F32),
        compiler_params=_cparams(3),
        name="experts",
    )(counts, xn, gate, dest, h, w_gate, w_up, w_down, final_g.reshape(1, -1))


def _rope_tables(pos):
    half = HEAD_DIM // 2
    inv = jnp.power(ROPE_THETA, -jnp.arange(half, dtype=F32) / half)
    ang = pos.astype(F32)[:, None] * inv[None, :]
    cos, sin = jnp.cos(ang), jnp.sin(ang)
    cos_t = jnp.concatenate([cos, cos, cos, cos], axis=1)
    sin_t = jnp.concatenate([-sin, sin, -sin, sin], axis=1)
    return cos_t, sin_t


def _rope_apply(y, cos_t, sin_t):
    half = HEAD_DIM // 2
    first = (lax.broadcasted_iota(jnp.int32, (y.shape[0], LANES), 1) % HEAD_DIM) < half
    cols = []
    for c in range(y.shape[1] // LANES):
        yc = y[:, c * LANES:(c + 1) * LANES]
        swapped = jnp.where(first, pltpu.roll(yc, LANES - half, 1), pltpu.roll(yc, half, 1))
        cols.append(yc * cos_t + swapped * sin_t)
    return jnp.concatenate(cols, axis=1)


def _qkv_kernel(x_ref, g_ref, w_ref, cos_ref, sin_ref, q_ref, k_ref, v_ref, wbf_ref, *, head_major):
    @pl.when((pl.program_id(0) == 0) & (pl.program_id(1) == 0))
    def _():
        wbf_ref[...] = w_ref[...].astype(BF16)

    xn = _rms(x_ref[...], g_ref[...]).astype(BF16)
    y = jnp.dot(xn, wbf_ref[...], preferred_element_type=F32)
    cos_t, sin_t = cos_ref[...], sin_ref[...]
    q = _rope_apply(y[:, :D_MODEL], cos_t, sin_t) * (HEAD_DIM ** -0.5)
    k = _rope_apply(y[:, D_MODEL:2 * D_MODEL], cos_t, sin_t)
    v = y[:, 2 * D_MODEL:]
    if head_major:
        tm = q.shape[0]
        for h in range(ATT_HEADS):
            sl = slice(h * HEAD_DIM, (h + 1) * HEAD_DIM)
            q_ref[h] = q[:, sl]
            for p in range(tm // PAGE_SIZE):
                rows = slice(p * PAGE_SIZE, (p + 1) * PAGE_SIZE)
                k_ref[p, h] = k[rows, sl]
                v_ref[p, h] = v[rows, sl]
    else:
        q_ref[...] = q
        k_ref[...] = k
        v_ref[...] = v


def _qkv_rope(x, g, w_qkv, pos, tm, head_major):
    b, t, _ = x.shape
    cos_t, sin_t = _rope_tables(pos)
    fixed = lambda i, j: (0, 0)
    if head_major:
        npg = tm // PAGE_SIZE
        out_specs = [pl.BlockSpec((None, ATT_HEADS, tm, HEAD_DIM), lambda i, j: (i, 0, j, 0)),
                     pl.BlockSpec((None, npg, ATT_HEADS, PAGE_SIZE, HEAD_DIM), lambda i, j: (i, j, 0, 0, 0)),
                     pl.BlockSpec((None, npg, ATT_HEADS, PAGE_SIZE, HEAD_DIM), lambda i, j: (i, j, 0, 0, 0))]
        kv_shape = (b, t // PAGE_SIZE, ATT_HEADS, PAGE_SIZE, HEAD_DIM)
        out_shape = [jax.ShapeDtypeStruct((b, ATT_HEADS, t, HEAD_DIM), F32),
                     jax.ShapeDtypeStruct(kv_shape, F32), jax.ShapeDtypeStruct(kv_shape, F32)]
    else:
        out_specs = [pl.BlockSpec((None, tm, D_MODEL), lambda i, j: (i, j, 0))] * 3
        out_shape = [jax.ShapeDtypeStruct((b, t, D_MODEL), F32)] * 3
    return pl.pallas_call(
        functools.partial(_qkv_kernel, head_major=head_major),
        grid=(b, t // tm),
        in_specs=[pl.BlockSpec((None, tm, D_MODEL), lambda i, j: (i, j, 0)),
                  pl.BlockSpec((1, D_MODEL), fixed),
                  pl.BlockSpec(w_qkv.shape, fixed),
                  pl.BlockSpec((tm, LANES), lambda i, j: (j, 0)),
                  pl.BlockSpec((tm, LANES), lambda i, j: (j, 0))],
        out_specs=out_specs,
        out_shape=out_shape,
        scratch_shapes=[pltpu.VMEM(w_qkv.shape, BF16)],
        compiler_params=_cparams(2),
        name="qkv_rope",
    )(x, g.reshape(1, -1), w_qkv, cos_t, sin_t)


def _top_blocks(gate, valid, n_keep):
    nb = gate.shape[0]
    blk = lax.broadcasted_iota(jnp.int32, gate.shape, 0)
    rank = jnp.zeros(gate.shape, F32)
    for m in range(nb):
        g_m = gate[m:m + 1, :]
        beats = (g_m > gate) | ((g_m == gate) & (m < blk))
        rank = rank + jnp.where(beats & valid[m:m + 1, :], 1.0, 0.0)
    return jnp.where(valid & (rank < n_keep), 1.0, 0.0)


def _moba_prompt_kernel(q_ref, k_ref, v_ref, o_ref, *, t, heads_per_step):
    nb = t // MOBA_BLOCK
    neg = jnp.float32(-jnp.inf)
    outs = []
    for hh in range(heads_per_step):
        q = q_ref[hh]
        k = k_ref[:, hh].reshape(t, HEAD_DIM)
        v = v_ref[:, hh].reshape(t, HEAD_DIM)
        kmean = jnp.concatenate(
            [jnp.mean(k[n * MOBA_BLOCK:(n + 1) * MOBA_BLOCK, :], axis=0, keepdims=True) for n in range(nb)], axis=0)
        gate = lax.dot_general(kmean, q, (((1,), (1,)), ((), ())), precision=HIGHEST,
                               preferred_element_type=F32)
        own_of_q = lax.broadcasted_iota(jnp.int32, (nb, t), 1) // MOBA_BLOCK
        valid = lax.broadcasted_iota(jnp.int32, (nb, t), 0) < own_of_q
        sel_t = _top_blocks(gate, valid, MOBA_TOPK)
        pad_rows = 16 - nb
        sel_t = jnp.concatenate([sel_t, jnp.zeros((pad_rows, t), F32)], axis=0)
        eye = jnp.where(lax.broadcasted_iota(jnp.int32, (16, LANES), 0)
                        == lax.broadcasted_iota(jnp.int32, (16, LANES), 1), 1.0, 0.0)
        sel = _tdot(sel_t, eye)
        kb, vb = k.astype(BF16), v.astype(BF16)
        causal = (lax.broadcasted_iota(jnp.int32, (MOBA_BLOCK, MOBA_BLOCK), 1)
                  <= lax.broadcasted_iota(jnp.int32, (MOBA_BLOCK, MOBA_BLOCK), 0))
        o_blocks = []
        for own in range(nb):
            rows = slice(own * MOBA_BLOCK, (own + 1) * MOBA_BLOCK)
            qb = q[rows, :].astype(BF16)
            scores = []
            for n in range(own):
                s = _bdot_t(qb, kb[n * MOBA_BLOCK:(n + 1) * MOBA_BLOCK, :])
                scores.append(jnp.where(sel[rows, n:n + 1] > 0.0, s, neg))
            scores.append(jnp.where(causal, _bdot_t(qb, kb[rows, :]), neg))
            m = scores[0].max(axis=-1, keepdims=True)
            for s in scores[1:]:
                m = jnp.maximum(m, s.max(axis=-1, keepdims=True))
            den = jnp.zeros((MOBA_BLOCK, 1), F32)
            acc = jnp.zeros((MOBA_BLOCK, HEAD_DIM), F32)
            for n, s in enumerate(scores):
                p = jnp.exp(s - m)
                den = den + p.sum(axis=-1, keepdims=True)
                acc = acc + jnp.dot(p.astype(BF16), vb[n * MOBA_BLOCK:(n + 1) * MOBA_BLOCK, :],
                                    preferred_element_type=F32)
            o_blocks.append(acc / den)
        outs.append(jnp.concatenate(o_blocks, axis=0))
    o_ref[...] = jnp.concatenate(outs, axis=1)


def _moba_prompt(q, k_rows, v_rows):
    b, n_h, t, _ = q.shape
    hps = LANES // HEAD_DIM
    npg = t // PAGE_SIZE
    kv_spec = pl.BlockSpec((None, npg, hps, PAGE_SIZE, HEAD_DIM), lambda i, j: (i, 0, j, 0, 0))
    return pl.pallas_call(
        functools.partial(_moba_prompt_kernel, t=t, heads_per_step=hps),
        grid=(b, n_h // hps),
        in_specs=[pl.BlockSpec((None, hps, t, HEAD_DIM), lambda i, j: (i, j, 0, 0)), kv_spec, kv_spec],
        out_specs=pl.BlockSpec((None, t, LANES), lambda i, j: (i, 0, j)),
        out_shape=jax.ShapeDtypeStruct((b, t, n_h * HEAD_DIM), F32),
        compiler_params=_cparams(2),
        name="moba_prompt",
    )(q, k_rows, v_rows)


def _kmean_select_kernel(pt_ref, q_ref, ptv_ref, *refs, pages_per_step, n_q, n_blk):
    page_refs = refs[:pages_per_step]
    sel_ref = refs[pages_per_step]
    kmt_ref = refs[pages_per_step + 1]
    s = pl.program_id(1)
    ppb = MOBA_BLOCK // PAGE_SIZE
    rows = ATT_HEADS * HEAD_DIM

    @pl.when(s == 0)
    def _():
        kmt_ref[...] = jnp.zeros_like(kmt_ref)

    lane = lax.broadcasted_iota(jnp.int32, (rows, LANES), 1)
    kmt = kmt_ref[...]
    for blk in range(pages_per_step // ppb):
        tot = page_refs[blk * ppb][...]
        for i in range(1, ppb):
            tot = tot + page_refs[blk * ppb + i][...]
        mean = tot.reshape(rows, PAGE_SIZE).sum(axis=-1, keepdims=True) * (1.0 / MOBA_BLOCK)
        kmt = jnp.where(lane == s * (pages_per_step // ppb) + blk, mean, kmt)
    kmt_ref[...] = kmt

    @pl.when(s == pl.num_programs(1) - 1)
    def _():
        q = q_ref[...]
        n_rows = q.shape[0]
        row_head = lax.broadcasted_iota(jnp.int32, (n_rows, LANES), 0) // n_q
        gate = jnp.zeros((n_rows, LANES), F32)
        for h in range(ATT_HEADS):
            g_h = jnp.dot(q, kmt[h * HEAD_DIM:(h + 1) * HEAD_DIM, :], precision=HIGHEST,
                          preferred_element_type=F32)
            gate = jnp.where(row_head == h, g_h, gate)
        blk_lane = lax.broadcasted_iota(jnp.int32, gate.shape, 1)
        neg = jnp.float32(-jnp.inf)
        gate = jnp.where(blk_lane < n_blk, gate, neg)
        pt_row = ptv_ref[...].astype(F32)
        pages = jnp.zeros(gate.shape, F32)
        for r in range(MOBA_TOPK):
            m = jnp.max(gate, axis=-1, keepdims=True)
            idx = jnp.min(jnp.where(gate == m, blk_lane, LANES), axis=-1, keepdims=True)
            for i in range(ppb):
                page = jnp.sum(jnp.where(blk_lane == idx * ppb + i, pt_row, 0.0), axis=-1, keepdims=True)
                pages = jnp.where(blk_lane == r * ppb + i, page, pages)
            gate = jnp.where(blk_lane == idx, neg, gate)
        sel_ref[...] = pages.astype(jnp.int32)


def _sample_select(q, cache_kt, page_table, n_blk, pages_per_step=8):
    db, n_h, n_q, _ = q.shape
    n_pages = page_table.shape[1]
    assert n_pages == LANES and n_blk <= LANES
    ppb = MOBA_BLOCK // PAGE_SIZE
    steps = n_blk * ppb // pages_per_step

    def page_spec(i):
        return pl.BlockSpec((None, n_h, HEAD_DIM, PAGE_SIZE),
                            lambda b, s, pt: (pt[b * n_pages + s * pages_per_step + i], 0, 0, 0))

    grid_spec = pltpu.PrefetchScalarGridSpec(
        num_scalar_prefetch=1,
        grid=(db, steps),
        in_specs=[pl.BlockSpec((None, n_h * n_q, HEAD_DIM), lambda b, s, pt: (b, 0, 0)),
                  pl.BlockSpec((None, 1, n_pages), lambda b, s, pt: (b, 0, 0))]
                 + [page_spec(i) for i in range(pages_per_step)],
        out_specs=pl.BlockSpec((None, n_h * n_q, LANES), lambda b, s, pt: (b, 0, 0)),
        scratch_shapes=[pltpu.VMEM((n_h * HEAD_DIM, LANES), F32)],
    )
    return pl.pallas_call(
        functools.partial(_kmean_select_kernel, pages_per_step=pages_per_step, n_q=n_q, n_blk=n_blk),
        grid_spec=grid_spec,
        out_shape=jax.ShapeDtypeStruct((db, n_h * n_q, LANES), jnp.int32),
        compiler_params=_cparams(2),
        name="sample_select",
    )(page_table.reshape(-1), q.reshape(db, n_h * n_q, HEAD_DIM), page_table.reshape(db, 1, n_pages),
      *([cache_kt] * pages_per_step))


def _sample_attn_kernel(pg_ref, q_ref, kn_ref, vn_ref, *refs, n_q, tiles_per_q):
    n_tiles = n_q * tiles_per_q
    k_refs = refs[:n_tiles]
    v_refs = refs[n_tiles:2 * n_tiles]
    o_ref = refs[2 * n_tiles]
    neg = jnp.float32(-jnp.inf)
    q = q_ref[...]
    kn, vn = kn_ref[...], vn_ref[...]
    rows8 = 8
    q8 = jnp.concatenate([q] * (rows8 // n_q), axis=0)
    kt_cat = jnp.concatenate([r[...] for r in k_refs], axis=1)
    vt_cat = jnp.concatenate([r[...] for r in v_refs], axis=1)
    s = _bdot(q8, kt_cat)
    row = lax.broadcasted_iota(jnp.int32, s.shape, 0) % n_q
    col = lax.broadcasted_iota(jnp.int32, s.shape, 1)
    s = jnp.where((col // (tiles_per_q * PAGE_SIZE)) == row, s, neg)
    row_o = lax.broadcasted_iota(jnp.int32, (rows8, 1), 0) % n_q
    s_own = [jnp.where(row_o >= c, jnp.sum(q8 * kn[c:c + 1, :], axis=-1, keepdims=True), neg)
             for c in range(n_q)]
    m = s.max(axis=-1, keepdims=True)
    for s_c in s_own:
        m = jnp.maximum(m, s_c)
    p = jnp.exp(s - m)
    den = p.sum(axis=-1, keepdims=True)
    acc = _bdot_t(p, vt_cat)
    for c, s_c in enumerate(s_own):
        p_c = jnp.exp(s_c - m)
        den = den + p_c
        acc = acc + p_c * vn[c:c + 1, :]
    o_ref[...] = (acc / den)[:n_q, :]


def _sample_attn(q, k_new, v_new, cache_kt, cache_vt, pages):
    db, n_h, n_q, _ = q.shape
    tiles_per_q = MOBA_TOPK * (MOBA_BLOCK // PAGE_SIZE)
    n_tiles = n_q * tiles_per_q

    def tile_spec(i):
        return pl.BlockSpec((None, None, HEAD_DIM, PAGE_SIZE),
                            lambda b, h, pg: (pg[(b * n_h + h) * n_tiles + i], h, 0, 0))

    small = pl.BlockSpec((None, None, n_q, HEAD_DIM), lambda b, h, pg: (b, h, 0, 0))
    grid_spec = pltpu.PrefetchScalarGridSpec(
        num_scalar_prefetch=1,
        grid=(db, n_h),
        in_specs=[small, small, small] + [tile_spec(i) for i in range(n_tiles)] * 2,
        out_specs=small,
    )
    return pl.pallas_call(
        functools.partial(_sample_attn_kernel, n_q=n_q, tiles_per_q=tiles_per_q),
        grid_spec=grid_spec,
        out_shape=jax.ShapeDtypeStruct((db, n_h, n_q, HEAD_DIM), F32),
        compiler_params=_cparams(2),
        name="sample_attn",
    )(pages, q, k_new, v_new, *([cache_kt] * n_tiles), *([cache_vt] * n_tiles))


def _even_layer(x, conv_prev, gla_prev, p, tm, conv_tt, gla_c, ffn_tm, ffn_tf):
    b, t, d = x.shape
    xf = x.reshape(b * t, d)
    glu, q, k, v, gate, la = _inproj(xf, p["norm_mix"], p["w_in"], p["w_a2"], p["b_a"], tm)
    as_bt = lambda a: a.reshape(b, t, a.shape[-1])
    conv_out, conv_state = _conv_module(as_bt(glu), conv_prev, p["conv_w"], p["conv_b"], p["ln_g"], p["ln_b"], conv_tt)
    o, gla_state = _gla(as_bt(q), as_bt(k), as_bt(la), as_bt(v), as_bt(gate),
                        gla_prev.reshape(b, GLA_HEADS * GLA_DK, GLA_DV), p["gla_norm_g"], gla_c, 2)
    h = _outproj(xf, [conv_out.reshape(b * t, -1), o.reshape(b * t, -1)], p["w_out"], tm)
    h = _ffn(h, p["norm_ffn"], p["ffn_w_gate"], p["ffn_w_up"], p["ffn_w_down"], ffn_tm, ffn_tf)
    return h.reshape(b, t, d), conv_state, gla_state.reshape(b, GLA_HEADS, GLA_DK, GLA_DV)


def kernel(x_prompt, x_sample, state_conv, state_gla, cache_k, cache_v, page_table, norm_mix_even, w_in_even, conv_w, conv_b, conv_ln_g, conv_ln_b, gla_w_a2, gla_b_a, gla_norm_g, w_out_even, norm_ffn_even, ffn_w_gate, ffn_w_up, ffn_w_down, norm_mix_odd, w_qkv_odd, w_out_odd, norm_ffn_odd, router_w, moe_w_gate, moe_w_up, moe_w_down, final_norm_g):
    bp, tp, d = x_prompt.shape
    db, ts, _ = x_sample.shape
    even = dict(norm_mix=norm_mix_even[0], w_in=w_in_even[0], conv_w=conv_w[0], conv_b=conv_b[0],
                ln_g=conv_ln_g[0], ln_b=conv_ln_b[0], w_a2=gla_w_a2[0], b_a=gla_b_a[0],
                gla_norm_g=gla_norm_g[0], w_out=w_out_even[0], norm_ffn=norm_ffn_even[0],
                ffn_w_gate=ffn_w_gate, ffn_w_up=ffn_w_up, ffn_w_down=ffn_w_down)

    hp, conv_p, gla_p = _even_layer(
        x_prompt, jnp.zeros((bp, CONV_W - 1, D_CONV), F32), jnp.zeros((bp, GLA_HEADS, GLA_DK, GLA_DV), F32),
        even, tm=512, conv_tt=256, gla_c=GLA_CHUNK, ffn_tm=1024, ffn_tf=256)

    m_s = db * ts
    xs = x_sample.reshape(m_s, d)
    glu, q, k, v, gate, la = _inproj(xs, even["norm_mix"], even["w_in"], even["w_a2"], even["b_a"], m_s)
    as_bt = lambda a: a.reshape(db, ts, a.shape[-1])
    conv_out_s, conv_s = _conv_module(as_bt(glu), state_conv[0], even["conv_w"], even["conv_b"],
                                      even["ln_g"], even["ln_b"], ts)
    pad_t = 16
    pad8 = lambda a: jnp.pad(as_bt(a), ((0, 0), (0, pad_t - ts), (0, 0)))
    o_s, gla_s = _gla(pad8(q), pad8(k), pad8(la), pad8(v), pad8(gate),
                      state_gla[0].reshape(db, GLA_HEADS * GLA_DK, GLA_DV), even["gla_norm_g"], pad_t, 4)
    hs = _outproj(xs, [conv_out_s.reshape(m_s, -1), o_s[:, :ts].reshape(m_s, -1)], even["w_out"], m_s)
    hs = _ffn(hs, even["norm_ffn"], ffn_w_gate, ffn_w_up, ffn_w_down, m_s, 256)
    gla_s = gla_s.reshape(db, GLA_HEADS, GLA_DK, GLA_DV)

    qp, k_rows, v_rows = _qkv_rope(hp, norm_mix_odd[0], w_qkv_odd[0], jnp.arange(tp), 256, True)
    op = _moba_prompt(qp, k_rows, v_rows)
    moe_wg, moe_wu, moe_wd = moe_w_gate[0].astype(BF16), moe_w_up[0].astype(BF16), moe_w_down[0].astype(BF16)
    hp2, xn_p, gate_p, dest_p = _route(hp.reshape(bp * tp, d), op.reshape(bp * tp, d), w_out_odd[0],
                                       norm_ffn_odd[0], router_w[0], 1024)
    y_prompt = _experts(hp2, xn_p, gate_p, dest_p, moe_wg, moe_wu, moe_wd, final_norm_g,
                        tm=1024, tf=1792, rb=128).reshape(bp, tp, d)

    n_blocks = page_table.shape[1] * PAGE_SIZE // MOBA_BLOCK
    past_len = page_table.shape[1] * PAGE_SIZE
    pos_s = past_len + (jnp.arange(m_s) % ts)
    qs, ks, vs = _qkv_rope(hs.reshape(1, m_s, d), norm_mix_odd[0], w_qkv_odd[0], pos_s, m_s, False)
    to_heads = lambda a: a.reshape(db, ts, ATT_HEADS, HEAD_DIM).transpose(0, 2, 1, 3)
    qs, ks, vs = to_heads(qs), to_heads(ks), to_heads(vs)
    cache_kt, cache_vt = jnp.swapaxes(cache_k[0], -1, -2), jnp.swapaxes(cache_v[0], -1, -2)
    tiles_per_q = MOBA_TOPK * (MOBA_BLOCK // PAGE_SIZE)
    pages = _sample_select(qs, cache_kt, page_table, n_blocks)[:, :, :tiles_per_q].reshape(-1)
    os_ = _sample_attn(qs, ks, vs, cache_kt, cache_vt, pages)
    os_ = os_.transpose(0, 2, 1, 3).reshape(m_s, d)
    hs2 = _outproj(hs, [os_], w_out_odd[0], m_s)
    y_sample = _ffn(hs2, norm_ffn_odd[0], moe_wg, moe_wu, moe_wd, m_s, 512,
                    router_w=router_w[0], final_g=final_norm_g).reshape(db, ts, d)

    return (y_prompt, y_sample, conv_p[None], gla_p[None], k_rows[None], v_rows[None],
            conv_s[None], gla_s[None], ks[None], vs[None])
```

```python
import functools

import jax
import jax.numpy as jnp
import numpy as np
from jax import lax
from jax.experimental import pallas as pl
from jax.experimental.pallas import tpu as pltpu

F32 = jnp.float32
BF16 = jnp.bfloat16
HIGHEST = lax.Precision.HIGHEST

LANES = 128
SUBLANES = 8
NORM_EPS = 1e-6
D_MODEL = 1024
D_CONV = 512
CONV_W = 31
CONV_HALO = 32
GLA_HEADS = 4
GLA_DK = 64
GLA_DV = 128
GLA_TAU = 16.0
GLA_CHUNK = 64
ATT_HEADS = 16
HEAD_DIM = 64
MOBA_BLOCK = 256
MOBA_TOPK = 3
PAGE_SIZE = 128
ROPE_THETA = 10000.0
N_EXPERTS = 8
VMEM_LIMIT = 60 * 1024 * 1024


def _cparams(n_axes):
    return pltpu.CompilerParams(dimension_semantics=("arbitrary",) * n_axes,
                                vmem_limit_bytes=VMEM_LIMIT)


def _rms(x, g):
    return x * lax.rsqrt(jnp.mean(x * x, axis=-1, keepdims=True) + NORM_EPS) * g


def _silu(x):
    return x * jax.nn.sigmoid(x)


def _bdot(a, b):
    return jnp.dot(a.astype(BF16), b.astype(BF16), preferred_element_type=F32)


def _bdot_t(a, b):
    return lax.dot_general(a.astype(BF16), b.astype(BF16), (((1,), (1,)), ((), ())),
                           preferred_element_type=F32)


def _tdot(a, b):
    return lax.dot_general(a.astype(BF16), b.astype(BF16), (((0,), (0,)), ((), ())),
                           preferred_element_type=F32)


def _inproj_kernel(x_ref, g_ref, w_ref, wa1_ref, wa2_ref, ba_ref,
                   glu_ref, q_ref, k_ref, v_ref, gate_ref, la_ref, wbf_ref, wa1bf_ref):
    @pl.when(pl.program_id(0) == 0)
    def _():
        wbf_ref[...] = w_ref[...].astype(BF16)
        wa1bf_ref[...] = wa1_ref[...].astype(BF16)

    xn = _rms(x_ref[...], g_ref[...]).astype(BF16)
    y = jnp.dot(xn, wbf_ref[...], preferred_element_type=F32)
    glu_ref[...] = y[:, :D_CONV] * jax.nn.sigmoid(y[:, D_CONV:2 * D_CONV])
    q_ref[...] = y[:, 1024:1280] * (GLA_DK ** -0.5)
    k_ref[...] = y[:, 1280:1536]
    v_ref[...] = y[:, 1536:2048]
    gate_ref[...] = y[:, 2048:2560]
    a = jnp.dot(xn, wa1bf_ref[...], preferred_element_type=F32)
    z = jnp.dot(a, wa2_ref[...], precision=HIGHEST, preferred_element_type=F32) + ba_ref[...]
    log_sig = jnp.minimum(z, 0.0) - jnp.log(1.0 + jnp.exp(-jnp.abs(z)))
    la_ref[...] = log_sig * (1.0 / GLA_TAU)


def _inproj(x, g, w_in, w_a2, b_a, tm):
    m = x.shape[0]
    n_main = 2560
    wa1 = jnp.pad(w_in[:, n_main:], ((0, 0), (0, LANES - (w_in.shape[1] - n_main))))
    wa2 = jnp.pad(w_a2, ((0, LANES - w_a2.shape[0]), (0, 0)))
    row = lambda i: (i, 0)
    fixed = lambda i: (0, 0)
    outs = [(D_CONV, F32), (256, F32), (256, F32), (512, F32), (512, F32), (256, F32)]
    return pl.pallas_call(
        _inproj_kernel,
        grid=(m // tm,),
        in_specs=[pl.BlockSpec((tm, D_MODEL), row),
                  pl.BlockSpec((1, D_MODEL), fixed),
                  pl.BlockSpec((D_MODEL, n_main), fixed),
                  pl.BlockSpec((D_MODEL, LANES), fixed),
                  pl.BlockSpec((LANES, 256), fixed),
                  pl.BlockSpec((1, 256), fixed)],
        out_specs=[pl.BlockSpec((tm, n), row) for n, _ in outs],
        out_shape=[jax.ShapeDtypeStruct((m, n), dt) for n, dt in outs],
        scratch_shapes=[pltpu.VMEM((D_MODEL, n_main), BF16), pltpu.VMEM((D_MODEL, LANES), BF16)],
        compiler_params=_cparams(1),
        name="inproj",
    )(x, g.reshape(1, -1), w_in, wa1, wa2, b_a.reshape(1, -1))


def _conv_kernel(glu_ref, prev_ref, w_ref, b_ref, lg_ref, lb_ref, out_ref, st_ref, full_ref, ph_ref, *, tt, rc):
    t = pl.program_id(1)
    lo = CONV_HALO - (CONV_W - 1)

    @pl.when(t == 0)
    def _():
        full_ref[0:lo, :] = jnp.zeros((lo, D_CONV), F32)
        full_ref[lo:CONV_HALO, :] = prev_ref[...]

    full_ref[CONV_HALO:CONV_HALO + tt, :] = glu_ref[...]
    span = ph_ref.shape[1]
    for r in range(1, SUBLANES):
        ph_ref[r - 1] = full_ref[r:r + span, :]

    def slab(c, j):
        a, r = divmod(lo + j, SUBLANES)
        start = c * rc + a * SUBLANES
        return full_ref[start:start + rc, :] if r == 0 else ph_ref[r - 1, start:start + rc, :]

    for c in range(tt // rc):
        acc = slab(c, 0) * w_ref[0:1, :]
        for j in range(1, CONV_W):
            acc = acc + slab(c, j) * w_ref[j:j + 1, :]
        y = acc + b_ref[...]
        mu = jnp.mean(y, axis=-1, keepdims=True)
        d = y - mu
        var = jnp.mean(d * d, axis=-1, keepdims=True)
        yn = d * lax.rsqrt(var + NORM_EPS) * lg_ref[...] + lb_ref[...]
        out_ref[c * rc:(c + 1) * rc, :] = _silu(yn).astype(out_ref.dtype)

    tail = full_ref[tt + lo:tt + CONV_HALO, :]

    @pl.when(t == pl.num_programs(1) - 1)
    def _():
        st_ref[...] = tail

    full_ref[lo:CONV_HALO, :] = tail


def _conv_module(glu, prev, conv_w, conv_b, ln_g, ln_b, tt):
    b, t, _ = glu.shape
    rc = min(tt, 32)
    fixed = lambda i, j: (0, 0)
    return pl.pallas_call(
        functools.partial(_conv_kernel, tt=tt, rc=rc),
        grid=(b, t // tt),
        in_specs=[pl.BlockSpec((None, tt, D_CONV), lambda i, j: (i, j, 0)),
                  pl.BlockSpec((None, CONV_W - 1, D_CONV), lambda i, j: (i, 0, 0)),
                  pl.BlockSpec((CONV_W, D_CONV), fixed),
                  pl.BlockSpec((1, D_CONV), fixed),
                  pl.BlockSpec((1, D_CONV), fixed),
                  pl.BlockSpec((1, D_CONV), fixed)],
        out_specs=[pl.BlockSpec((None, tt, D_CONV), lambda i, j: (i, j, 0)),
                   pl.BlockSpec((None, CONV_W - 1, D_CONV), lambda i, j: (i, 0, 0))],
        out_shape=[jax.ShapeDtypeStruct((b, t, D_CONV), BF16),
                   jax.ShapeDtypeStruct((b, CONV_W - 1, D_CONV), F32)],
        scratch_shapes=[pltpu.VMEM((CONV_HALO + tt, D_CONV), F32),
                        pltpu.VMEM((SUBLANES - 1, CONV_HALO + tt - SUBLANES, D_CONV), F32)],
        compiler_params=_cparams(2),
        name="conv_module",
    )(glu, prev, conv_w, conv_b.reshape(1, -1), ln_g.reshape(1, -1), ln_b.reshape(1, -1))


def _gla_levels(c):
    levels, g_big = [], c
    while g_big > 1:
        g_small = max(g_big // 4, 1)
        levels.append((g_big, g_small))
        g_big = g_small
    return levels


def _gla_decay_matrices(c):
    t = np.arange(c)[:, None]
    s = np.arange(c)[None, :]
    mats = [(s <= t), (s > t)]
    for g_big, g_small in _gla_levels(c):
        sub = (t % g_big) // g_small
        ref_q = (t // g_big) * g_big + sub * g_small - 1
        mats.append((sub >= 1) & (s > ref_q) & (s <= t))
        for i in range(1, g_big // g_small):
            ref_k = (t // g_big) * g_big + i * g_small - 1
            mats.append(((t % g_big) < i * g_small) & (s > t) & (s <= ref_k))
    return np.concatenate([m.astype(np.float32) for m in mats], axis=0)


def _gla_kernel(q_ref, k_ref, la_ref, v_ref, gate_ref, s0_ref, ng_ref, dmat_ref,
                out_ref, st_ref, s_ref, *, c):
    ci = pl.program_id(1)

    @pl.when(ci == 0)
    def _():
        s_ref[...] = s0_ref[...]

    for bi in range(q_ref.shape[0]):
        out, new_state = _gla_chunk(q_ref[bi], k_ref[bi], la_ref[bi], v_ref[bi], gate_ref[bi], s_ref[bi],
                                    ng_ref[...], dmat_ref[...], c)
        out_ref[bi] = out.astype(out_ref.dtype)
        s_ref[bi] = new_state

    @pl.when(ci == pl.num_programs(1) - 1)
    def _():
        st_ref[...] = s_ref[...]


def _gla_chunk(q, k, la, v, gate, state, norm_g, dmat, c):
    hk = GLA_HEADS * GLA_DK
    la1 = la.astype(BF16)
    r1 = la - la1.astype(F32)
    la2 = r1.astype(BF16)
    la3 = (r1 - la2.astype(F32)).astype(BF16)
    e_all = (jnp.dot(dmat, la1, preferred_element_type=F32)
             + jnp.dot(dmat, la2, preferred_element_type=F32)
             + jnp.dot(dmat, la3, preferred_element_type=F32))

    def e_blk(i):
        return e_all[i * c:(i + 1) * c, :]

    lane_head = lax.broadcasted_iota(jnp.int32, (c, hk), 1) // GLA_DK
    head_masks = [(lane_head == h).astype(F32) for h in range(GLA_HEADS)]

    def stack_heads(x):
        return jnp.concatenate([x * m for m in head_masks], axis=0).astype(BF16)

    row_t = lax.broadcasted_iota(jnp.int32, (GLA_HEADS * c, c), 0) % c
    col_s = lax.broadcasted_iota(jnp.int32, (GLA_HEADS * c, c), 1)
    row_k = lax.broadcasted_iota(jnp.int32, (c, hk), 0)

    o_st = jnp.dot(stack_heads(q * jnp.exp(e_blk(0))), state.astype(BF16),
                   preferred_element_type=F32)

    att = jnp.where(row_t == col_s, _bdot_t(stack_heads(q), k), 0.0)
    idx = 2
    for g_big, g_small in _gla_levels(c):
        q_lvl = stack_heads(q * jnp.exp(e_blk(idx)))
        idx += 1
        same_blk = (row_t // g_big) == (col_s // g_big)
        sub_t = (row_t % g_big) // g_small
        for i in range(1, g_big // g_small):
            k_i = jnp.where((row_k % g_big) < i * g_small, k * jnp.exp(e_blk(idx)), 0.0)
            idx += 1
            p = _bdot_t(q_lvl, k_i)
            att = att + jnp.where(same_blk & (sub_t == i), p, 0.0)

    att = att.astype(BF16)
    outs = []
    for h in range(GLA_HEADS):
        v_h = v[:, h * GLA_DV:(h + 1) * GLA_DV]
        o_h = o_st[h * c:(h + 1) * c, :] + jnp.dot(att[h * c:(h + 1) * c, :], v_h.astype(BF16),
                                                   preferred_element_type=F32)
        o_h = _rms(o_h, norm_g[:, h * GLA_DV:(h + 1) * GLA_DV])
        outs.append(o_h * _silu(gate[:, h * GLA_DV:(h + 1) * GLA_DV]))

    kd = (k * jnp.exp(e_blk(1))).astype(BF16)
    ones = jnp.ones((c, GLA_DV), BF16)
    b_last = _tdot(la1, ones) + _tdot(la2, ones) + _tdot(la3, ones)
    row_head = lax.broadcasted_iota(jnp.int32, (hk, GLA_DV), 0) // GLA_DK
    upd = jnp.zeros((hk, GLA_DV), F32)
    for h in range(GLA_HEADS):
        kv = _tdot(kd, v[:, h * GLA_DV:(h + 1) * GLA_DV])
        upd = upd + jnp.where(row_head == h, kv, 0.0)
    return jnp.concatenate(outs, axis=1), jnp.exp(b_last) * state + upd


def _gla(q, k, la, v, gate, s0, norm_g, c, nbs):
    b, t, hk = q.shape
    dv_all = v.shape[-1]
    dmat = jnp.asarray(_gla_decay_matrices(c), dtype=BF16)
    tok = lambda n: pl.BlockSpec((nbs, c, n), lambda i, j: (i, j, 0))
    fixed = lambda i, j: (0, 0)
    return pl.pallas_call(
        functools.partial(_gla_kernel, c=c),
        grid=(b // nbs, t // c),
        in_specs=[tok(hk), tok(hk), tok(hk), tok(dv_all), tok(dv_all),
                  pl.BlockSpec((nbs, hk, GLA_DV), lambda i, j: (i, 0, 0)),
                  pl.BlockSpec((1, dv_all), fixed),
                  pl.BlockSpec(dmat.shape, fixed)],
        out_specs=[tok(dv_all), pl.BlockSpec((nbs, hk, GLA_DV), lambda i, j: (i, 0, 0))],
        out_shape=[jax.ShapeDtypeStruct((b, t, dv_all), BF16),
                   jax.ShapeDtypeStruct((b, hk, GLA_DV), F32)],
        scratch_shapes=[pltpu.VMEM((nbs, hk, GLA_DV), F32)],
        compiler_params=_cparams(2),
        name="gla",
    )(q, k, la, v, gate, s0, norm_g.reshape(1, -1), dmat)


def _outproj_kernel(*refs, n_parts):
    x_ref = refs[0]
    a_refs = refs[1:1 + n_parts]
    w_ref = refs[1 + n_parts]
    out_ref = refs[2 + n_parts]
    wbf_ref = refs[3 + n_parts]

    @pl.when(pl.program_id(0) == 0)
    def _():
        wbf_ref[...] = w_ref[...].astype(BF16)

    acc = x_ref[...]
    off = 0
    for a_ref in a_refs:
        n = a_ref.shape[-1]
        acc = acc + jnp.dot(a_ref[...].astype(BF16), wbf_ref[off:off + n, :], preferred_element_type=F32)
        off += n
    out_ref[...] = acc


def _outproj(x, parts, w, tm):
    m = x.shape[0]
    row = lambda i: (i, 0)
    return pl.pallas_call(
        functools.partial(_outproj_kernel, n_parts=len(parts)),
        grid=(m // tm,),
        in_specs=[pl.BlockSpec((tm, D_MODEL), row)]
                 + [pl.BlockSpec((tm, p.shape[-1]), row) for p in parts]
                 + [pl.BlockSpec(w.shape, lambda i: (0, 0))],
        out_specs=pl.BlockSpec((tm, D_MODEL), row),
        out_shape=jax.ShapeDtypeStruct((m, D_MODEL), F32),
        scratch_shapes=[pltpu.VMEM(w.shape, BF16)],
        compiler_params=_cparams(1),
        name="outproj",
    )(x, *parts, w)


def _ffn_kernel(*refs, routed, final_norm):
    it = iter(refs)
    h_ref, g_ref = next(it), next(it)
    rw_ref = next(it) if routed else None
    wg_ref, wu_ref, wd_ref = next(it), next(it), next(it)
    fg_ref = next(it) if final_norm else None
    out_ref, xn_ref, acc_ref = next(it), next(it), next(it)
    gate_ref = next(it) if routed else None
    e, f = pl.program_id(1), pl.program_id(2)

    @pl.when((e == 0) & (f == 0))
    def _():
        xn = _rms(h_ref[...], g_ref[...])
        xn_ref[...] = xn.astype(BF16)
        acc_ref[...] = jnp.zeros_like(acc_ref)
        if routed:
            gate_ref[...] = _top2_gates(jnp.dot(xn, rw_ref[...], precision=HIGHEST, preferred_element_type=F32))

    xn = xn_ref[...]
    hid = _silu(jnp.dot(xn, wg_ref[...].astype(BF16), preferred_element_type=F32)) \
        * jnp.dot(xn, wu_ref[...].astype(BF16), preferred_element_type=F32)
    if routed:
        lane = lax.broadcasted_iota(jnp.int32, gate_ref.shape, 1)
        hid = hid * jnp.sum(jnp.where(lane == e, gate_ref[...], 0.0), axis=-1, keepdims=True)
    acc_ref[...] += jnp.dot(hid.astype(BF16), wd_ref[...].astype(BF16), preferred_element_type=F32)

    @pl.when((e == pl.num_programs(1) - 1) & (f == pl.num_programs(2) - 1))
    def _():
        y = h_ref[...] + acc_ref[...]
        if final_norm:
            y = _rms(y, fg_ref[...])
        out_ref[...] = y


def _ffn(h, g, w_gate, w_up, w_down, tm, tf, router_w=None, final_g=None):
    m = h.shape[0]
    n_e, _, d_ff = w_gate.shape
    routed = router_w is not None
    final_norm = final_g is not None
    row = lambda i, e, f: (i, 0)
    fixed = lambda i, e, f: (0, 0)
    args = [h, g.reshape(1, -1)]
    in_specs = [pl.BlockSpec((tm, D_MODEL), row), pl.BlockSpec((1, D_MODEL), fixed)]
    if routed:
        args.append(jnp.pad(router_w, ((0, 0), (0, LANES - router_w.shape[1]))))
        in_specs.append(pl.BlockSpec((D_MODEL, LANES), fixed))
    args += [w_gate, w_up, w_down]
    in_specs += [pl.BlockSpec((None, D_MODEL, tf), lambda i, e, f: (e, 0, f)),
                 pl.BlockSpec((None, D_MODEL, tf), lambda i, e, f: (e, 0, f)),
                 pl.BlockSpec((None, tf, D_MODEL), lambda i, e, f: (e, f, 0))]
    if final_norm:
        args.append(final_g.reshape(1, -1))
        in_specs.append(pl.BlockSpec((1, D_MODEL), fixed))
    scratch = [pltpu.VMEM((tm, D_MODEL), BF16), pltpu.VMEM((tm, D_MODEL), F32)]
    if routed:
        scratch.append(pltpu.VMEM((tm, LANES), F32))
    return pl.pallas_call(
        functools.partial(_ffn_kernel, routed=routed, final_norm=final_norm),
        grid=(m // tm, n_e, d_ff // tf),
        in_specs=in_specs,
        out_specs=pl.BlockSpec((tm, D_MODEL), row),
        out_shape=jax.ShapeDtypeStruct((m, D_MODEL), F32),
        scratch_shapes=scratch,
        compiler_params=_cparams(3),
        name="moe_ffn" if routed else "ffn",
    )(*args)


def _top2_gates(logits):
    lane = lax.broadcasted_iota(jnp.int32, logits.shape, 1)
    neg = jnp.float32(-jnp.inf)
    logits = jnp.where(lane < N_EXPERTS, logits, neg)
    m1 = jnp.max(logits, axis=-1, keepdims=True)
    i1 = jnp.min(jnp.where(logits == m1, lane, LANES), axis=-1, keepdims=True)
    rest = jnp.where(lane == i1, neg, logits)
    m2 = jnp.max(rest, axis=-1, keepdims=True)
    i2 = jnp.min(jnp.where(rest == m2, lane, LANES), axis=-1, keepdims=True)
    e2 = jnp.exp(m2 - m1)
    g1 = 1.0 / (1.0 + e2)
    g2 = e2 / (1.0 + e2)
    return jnp.where(lane == i1, g1, 0.0) + jnp.where(lane == i2, g2, 0.0)


def _route_kernel(x_ref, a_ref, w_ref, g_ref, rw_ref, h_ref, xn_ref, gate_ref, dest_ref, wbf_ref, ltri_ref):
    tm = x_ref.shape[0]

    @pl.when(pl.program_id(0) == 0)
    def _():
        wbf_ref[...] = w_ref[...].astype(BF16)
        earlier = (lax.broadcasted_iota(jnp.int32, (tm, tm), 1) < lax.broadcasted_iota(jnp.int32, (tm, tm), 0))
        ltri_ref[...] = jnp.where(earlier, 1.0, 0.0).astype(BF16)

    h = x_ref[...] + jnp.dot(a_ref[...].astype(BF16), wbf_ref[...], preferred_element_type=F32)
    h_ref[...] = h
    xn = _rms(h, g_ref[...])
    xn_ref[...] = xn.astype(BF16)
    gate = _top2_gates(jnp.dot(xn, rw_ref[...], precision=HIGHEST, preferred_element_type=F32))
    gate_ref[...] = gate
    routed = gate > 0.0
    rank = jnp.dot(ltri_ref[...], jnp.where(routed, 1.0, 0.0).astype(BF16), preferred_element_type=F32)
    dest_ref[...] = jnp.where(routed, rank, -1.0)


def _route(x, a, w_out, g, router_w, tm):
    m = x.shape[0]
    row = lambda i: (i, 0)
    fixed = lambda i: (0, 0)
    rw = jnp.pad(router_w, ((0, 0), (0, LANES - router_w.shape[1])))
    return pl.pallas_call(
        _route_kernel,
        grid=(m // tm,),
        in_specs=[pl.BlockSpec((tm, D_MODEL), row), pl.BlockSpec((tm, D_MODEL), row),
                  pl.BlockSpec(w_out.shape, fixed), pl.BlockSpec((1, D_MODEL), fixed),
                  pl.BlockSpec((D_MODEL, LANES), fixed)],
        out_specs=[pl.BlockSpec((tm, D_MODEL), row), pl.BlockSpec((tm, D_MODEL), row),
                   pl.BlockSpec((tm, LANES), row), pl.BlockSpec((tm, LANES), row)],
        out_shape=[jax.ShapeDtypeStruct((m, D_MODEL), F32), jax.ShapeDtypeStruct((m, D_MODEL), BF16),
                   jax.ShapeDtypeStruct((m, LANES), F32), jax.ShapeDtypeStruct((m, LANES), F32)],
        scratch_shapes=[pltpu.VMEM(w_out.shape, BF16), pltpu.VMEM((tm, tm), BF16)],
        compiler_params=_cparams(1),
        name="route",
    )(x, a, w_out, g.reshape(1, -1), rw)


def _experts_kernel(cnt_ref, xn_ref, gate_ref, dest_ref, h_ref, wg_ref, wu_ref, wd_ref, fg_ref, out_ref,
                    dest_t_ref, xc_ref, yc_ref, *, rb):
    i, e, f = pl.program_id(0), pl.program_id(1), pl.program_id(2)
    n_e, n_f = pl.num_programs(1), pl.num_programs(2)
    tm = xn_ref.shape[0]

    @pl.when((e == 0) & (f == 0))
    def _():
        dest_t_ref[...] = jnp.transpose(dest_ref[...])
        out_ref[...] = h_ref[...]

    n_blk = (cnt_ref[i * n_e + e] + rb - 1) // rb
    slot_row = dest_t_ref[pl.ds(e, 1), :]
    lane = lax.broadcasted_iota(jnp.int32, (tm, LANES), 1)
    slot_col = jnp.sum(jnp.where(lane == e, dest_ref[...], 0.0), axis=-1, keepdims=True)

    def rows(blk):
        return pl.ds(pl.multiple_of(blk * rb, rb), rb)

    mb = 2 * rb
    n_mv = (n_blk + 1) // 2

    def mrows(blk):
        return pl.ds(pl.multiple_of(blk * mb, mb), mb)

    @pl.when(f == 0)
    def _():
        def gather(blk, carry):
            want = (blk * mb + lax.broadcasted_iota(jnp.int32, (mb, tm), 0)).astype(F32)
            pick = jnp.where(slot_row == want, 1.0, 0.0).astype(BF16)
            xc_ref[mrows(blk), :] = jnp.dot(pick, xn_ref[...], preferred_element_type=F32).astype(BF16)
            yc_ref[mrows(blk), :] = jnp.zeros((mb, D_MODEL), F32)
            return carry
        lax.fori_loop(0, n_mv, gather, 0)

    def expert(blk, carry):
        xc = xc_ref[rows(blk), :]
        hid = _silu(jnp.dot(xc, wg_ref[...], preferred_element_type=F32)) \
            * jnp.dot(xc, wu_ref[...], preferred_element_type=F32)
        yc_ref[rows(blk), :] += jnp.dot(hid.astype(BF16), wd_ref[...], preferred_element_type=F32)
        return carry
    lax.fori_loop(0, n_blk, expert, 0)

    @pl.when(f == n_f - 1)
    def _():
        gate_col = jnp.sum(jnp.where(lane == e, gate_ref[...], 0.0), axis=-1, keepdims=True)

        def scatter(blk, carry):
            want = (blk * mb + lax.broadcasted_iota(jnp.int32, (tm, mb), 1)).astype(F32)
            place = jnp.where(slot_col == want, 1.0, 0.0).astype(BF16)
            out_ref[...] += gate_col * jnp.dot(place, yc_ref[mrows(blk), :].astype(BF16),
                                               preferred_element_type=F32)
            return carry
        lax.fori_loop(0, n_mv, scatter, 0)

    @pl.when((e == n_e - 1) & (f == n_f - 1))
    def _():
        out_ref[...] = _rms(out_ref[...], fg_ref[...])


def _experts(h, xn, gate, dest, w_gate, w_up, w_down, final_g, tm, tf, rb):
    m = h.shape[0]
    n_e, _, d_ff = w_gate.shape
    assert tm % (2 * rb) == 0 and m % tm == 0 and d_ff % tf == 0
    counts = jnp.sum((dest[:, :n_e] >= 0.0).reshape(m // tm, tm, n_e), axis=1, dtype=jnp.int32).reshape(-1)
    row = lambda i, e, f, c: (i, 0)
    grid_spec = pltpu.PrefetchScalarGridSpec(
        num_scalar_prefetch=1,
        grid=(m // tm, n_e, d_ff // tf),
        in_specs=[pl.BlockSpec((tm, D_MODEL), row), pl.BlockSpec((tm, LANES), row), pl.BlockSpec((tm, LANES), row),
                  pl.BlockSpec((tm, D_MODEL), row),
                  pl.BlockSpec((None, D_MODEL, tf), lambda i, e, f, c: (e, 0, f)),
                  pl.BlockSpec((None, D_MODEL, tf), lambda i, e, f, c: (e, 0, f)),
                  pl.BlockSpec((None, tf, D_MODEL), lambda i, e, f, c: (e, f, 0)),
                  pl.BlockSpec((1, D_MODEL), lambda i, e, f, c: (0, 0))],
        out_specs=pl.BlockSpec((tm, D_MODEL), row),
        scratch_shapes=[pltpu.VMEM((LANES, tm), F32), pltpu.VMEM((tm, D_MODEL), BF16),
                        pltpu.VMEM((tm, D_MODEL), F32)],
    )
    return pl.pallas_call(
        functools.partial(_experts_kernel, rb=rb),
        grid_spec=grid_spec,
        out_shape=jax.ShapeDtypeStruct((m, D_MODEL), F32),
        compiler_params=_cparams(3),
        name="experts",
    )(counts, xn, gate, dest, h, w_gate, w_up, w_down, final_g.reshape(1, -1))


def _rope_tables(pos):
    half = HEAD_DIM // 2
    inv = jnp.power(ROPE_THETA, -jnp.arange(half, dtype=F32) / half)
    ang = pos.astype(F32)[:, None] * inv[None, :]
    cos, sin = jnp.cos(ang), jnp.sin(ang)
    cos_t = jnp.concatenate([cos, cos, cos, cos], axis=1)
    sin_t = jnp.concatenate([-sin, sin, -sin, sin], axis=1)
    return cos_t, sin_t


def _rope_apply(y, cos_t, sin_t):
    half = HEAD_DIM // 2
    first = (lax.broadcasted_iota(jnp.int32, (y.shape[0], LANES), 1) % HEAD_DIM) < half
    cols = []
    for c in range(y.shape[1] // LANES):
        yc = y[:, c * LANES:(c + 1) * LANES]
        swapped = jnp.where(first, pltpu.roll(yc, LANES - half, 1), pltpu.roll(yc, half, 1))
        cols.append(yc * cos_t + swapped * sin_t)
    return jnp.concatenate(cols, axis=1)


def _qkv_kernel(x_ref, g_ref, w_ref, cos_ref, sin_ref, q_ref, k_ref, v_ref, wbf_ref, *, head_major):
    @pl.when((pl.program_id(0) == 0) & (pl.program_id(1) == 0))
    def _():
        wbf_ref[...] = w_ref[...].astype(BF16)

    xn = _rms(x_ref[...], g_ref[...]).astype(BF16)
    y = jnp.dot(xn, wbf_ref[...], preferred_element_type=F32)
    cos_t, sin_t = cos_ref[...], sin_ref[...]
    q = _rope_apply(y[:, :D_MODEL], cos_t, sin_t) * (HEAD_DIM ** -0.5)
    k = _rope_apply(y[:, D_MODEL:2 * D_MODEL], cos_t, sin_t)
    v = y[:, 2 * D_MODEL:]
    if head_major:
        tm = q.shape[0]
        for h in range(ATT_HEADS):
            sl = slice(h * HEAD_DIM, (h + 1) * HEAD_DIM)
            q_ref[h] = q[:, sl]
            for p in range(tm // PAGE_SIZE):
                rows = slice(p * PAGE_SIZE, (p + 1) * PAGE_SIZE)
                k_ref[p, h] = k[rows, sl]
                v_ref[p, h] = v[rows, sl]
    else:
        q_ref[...] = q
        k_ref[...] = k
        v_ref[...] = v


def _qkv_rope(x, g, w_qkv, pos, tm, head_major):
    b, t, _ = x.shape
    cos_t, sin_t = _rope_tables(pos)
    fixed = lambda i, j: (0, 0)
    if head_major:
        npg = tm // PAGE_SIZE
        out_specs = [pl.BlockSpec((None, ATT_HEADS, tm, HEAD_DIM), lambda i, j: (i, 0, j, 0)),
                     pl.BlockSpec((None, npg, ATT_HEADS, PAGE_SIZE, HEAD_DIM), lambda i, j: (i, j, 0, 0, 0)),
                     pl.BlockSpec((None, npg, ATT_HEADS, PAGE_SIZE, HEAD_DIM), lambda i, j: (i, j, 0, 0, 0))]
        kv_shape = (b, t // PAGE_SIZE, ATT_HEADS, PAGE_SIZE, HEAD_DIM)
        out_shape = [jax.ShapeDtypeStruct((b, ATT_HEADS, t, HEAD_DIM), F32),
                     jax.ShapeDtypeStruct(kv_shape, F32), jax.ShapeDtypeStruct(kv_shape, F32)]
    else:
        out_specs = [pl.BlockSpec((None, tm, D_MODEL), lambda i, j: (i, j, 0))] * 3
        out_shape = [jax.ShapeDtypeStruct((b, t, D_MODEL), F32)] * 3
    return pl.pallas_call(
        functools.partial(_qkv_kernel, head_major=head_major),
        grid=(b, t // tm),
        in_specs=[pl.BlockSpec((None, tm, D_MODEL), lambda i, j: (i, j, 0)),
                  pl.BlockSpec((1, D_MODEL), fixed),
                  pl.BlockSpec(w_qkv.shape, fixed),
                  pl.BlockSpec((tm, LANES), lambda i, j: (j, 0)),
                  pl.BlockSpec((tm, LANES), lambda i, j: (j, 0))],
        out_specs=out_specs,
        out_shape=out_shape,
        scratch_shapes=[pltpu.VMEM(w_qkv.shape, BF16)],
        compiler_params=_cparams(2),
        name="qkv_rope",
    )(x, g.reshape(1, -1), w_qkv, cos_t, sin_t)


def _top_blocks(gate, valid, n_keep):
    nb = gate.shape[0]
    blk = lax.broadcasted_iota(jnp.int32, gate.shape, 0)
    rank = jnp.zeros(gate.shape, F32)
    for m in range(nb):
        g_m = gate[m:m + 1, :]
        beats = (g_m > gate) | ((g_m == gate) & (m < blk))
        rank = rank + jnp.where(beats & valid[m:m + 1, :], 1.0, 0.0)
    return jnp.where(valid & (rank < n_keep), 1.0, 0.0)


def _moba_prompt_kernel(q_ref, k_ref, v_ref, o_ref, *, t, heads_per_step):
    nb = t // MOBA_BLOCK
    neg = jnp.float32(-jnp.inf)
    outs = []
    for hh in range(heads_per_step):
        q = q_ref[hh]
        k = k_ref[:, hh].reshape(t, HEAD_DIM)
        v = v_ref[:, hh].reshape(t, HEAD_DIM)
        kmean = jnp.concatenate(
            [jnp.mean(k[n * MOBA_BLOCK:(n + 1) * MOBA_BLOCK, :], axis=0, keepdims=True) for n in range(nb)], axis=0)
        gate = lax.dot_general(kmean, q, (((1,), (1,)), ((), ())), precision=HIGHEST,
                               preferred_element_type=F32)
        own_of_q = lax.broadcasted_iota(jnp.int32, (nb, t), 1) // MOBA_BLOCK
        valid = lax.broadcasted_iota(jnp.int32, (nb, t), 0) < own_of_q
        sel_t = _top_blocks(gate, valid, MOBA_TOPK)
        pad_rows = 16 - nb
        sel_t = jnp.concatenate([sel_t, jnp.zeros((pad_rows, t), F32)], axis=0)
        eye = jnp.where(lax.broadcasted_iota(jnp.int32, (16, LANES), 0)
                        == lax.broadcasted_iota(jnp.int32, (16, LANES), 1), 1.0, 0.0)
        sel = _tdot(sel_t, eye)
        kb = k.astype(BF16)
        vb = jnp.concatenate([v, jnp.ones((t, LANES - HEAD_DIM), F32)], axis=1).astype(BF16)
        causal = (lax.broadcasted_iota(jnp.int32, (MOBA_BLOCK, MOBA_BLOCK), 1)
                  <= lax.broadcasted_iota(jnp.int32, (MOBA_BLOCK, MOBA_BLOCK), 0))
        o_blocks = []
        for own in range(nb):
            rows = slice(own * MOBA_BLOCK, (own + 1) * MOBA_BLOCK)
            qb = q[rows, :].astype(BF16)
            scores = []
            for n in range(own):
                s = _bdot_t(qb, kb[n * MOBA_BLOCK:(n + 1) * MOBA_BLOCK, :])
                scores.append(jnp.where(sel[rows, n:n + 1] > 0.0, s, neg))
            scores.append(jnp.where(causal, _bdot_t(qb, kb[rows, :]), neg))
            m_elem = scores[0]
            for s in scores[1:]:
                m_elem = jnp.maximum(m_elem, s)
            m = m_elem.max(axis=-1, keepdims=True)
            acc = jnp.zeros((MOBA_BLOCK, LANES), F32)
            for n, s in enumerate(scores):
                acc = acc + jnp.dot(jnp.exp(s - m).astype(BF16), vb[n * MOBA_BLOCK:(n + 1) * MOBA_BLOCK, :],
                                    preferred_element_type=F32)
            o_blocks.append(acc[:, :HEAD_DIM] / acc[:, HEAD_DIM:HEAD_DIM + 1])
        outs.append(jnp.concatenate(o_blocks, axis=0))
    o_ref[...] = jnp.concatenate(outs, axis=1)


def _moba_prompt(q, k_rows, v_rows):
    b, n_h, t, _ = q.shape
    hps = LANES // HEAD_DIM
    npg = t // PAGE_SIZE
    kv_spec = pl.BlockSpec((None, npg, hps, PAGE_SIZE, HEAD_DIM), lambda i, j: (i, 0, j, 0, 0))
    return pl.pallas_call(
        functools.partial(_moba_prompt_kernel, t=t, heads_per_step=hps),
        grid=(b, n_h // hps),
        in_specs=[pl.BlockSpec((None, hps, t, HEAD_DIM), lambda i, j: (i, j, 0, 0)), kv_spec, kv_spec],
        out_specs=pl.BlockSpec((None, t, LANES), lambda i, j: (i, 0, j)),
        out_shape=jax.ShapeDtypeStruct((b, t, n_h * HEAD_DIM), F32),
        compiler_params=_cparams(2),
        name="moba_prompt",
    )(q, k_rows, v_rows)


def _kmean_select_kernel(pt_ref, q_ref, ptv_ref, *refs, pages_per_step, n_q, n_blk):
    page_refs = refs[:pages_per_step]
    sel_ref = refs[pages_per_step]
    kmt_ref = refs[pages_per_step + 1]
    s = pl.program_id(1)
    ppb = MOBA_BLOCK // PAGE_SIZE
    rows = ATT_HEADS * HEAD_DIM

    @pl.when(s == 0)
    def _():
        kmt_ref[...] = jnp.zeros_like(kmt_ref)

    lane = lax.broadcasted_iota(jnp.int32, (rows, LANES), 1)
    kmt = kmt_ref[...]
    for blk in range(pages_per_step // ppb):
        tot = page_refs[blk * ppb][...]
        for i in range(1, ppb):
            tot = tot + page_refs[blk * ppb + i][...]
        mean = tot.reshape(rows, PAGE_SIZE).sum(axis=-1, keepdims=True) * (1.0 / MOBA_BLOCK)
        kmt = jnp.where(lane == s * (pages_per_step // ppb) + blk, mean, kmt)
    kmt_ref[...] = kmt

    @pl.when(s == pl.num_programs(1) - 1)
    def _():
        q = q_ref[...]
        n_rows = q.shape[0]
        row_head = lax.broadcasted_iota(jnp.int32, (n_rows, LANES), 0) // n_q
        gate = jnp.zeros((n_rows, LANES), F32)
        for h in range(ATT_HEADS):
            g_h = jnp.dot(q, kmt[h * HEAD_DIM:(h + 1) * HEAD_DIM, :], precision=HIGHEST,
                          preferred_element_type=F32)
            gate = jnp.where(row_head == h, g_h, gate)
        blk_lane = lax.broadcasted_iota(jnp.int32, gate.shape, 1)
        neg = jnp.float32(-jnp.inf)
        gate = jnp.where(blk_lane < n_blk, gate, neg)
        pt_row = ptv_ref[...].astype(F32)
        pages = jnp.zeros(gate.shape, F32)
        for r in range(MOBA_TOPK):
            m = jnp.max(gate, axis=-1, keepdims=True)
            idx = jnp.min(jnp.where(gate == m, blk_lane, LANES), axis=-1, keepdims=True)
            for i in range(ppb):
                page = jnp.sum(jnp.where(blk_lane == idx * ppb + i, pt_row, 0.0), axis=-1, keepdims=True)
                pages = jnp.where(blk_lane == r * ppb + i, page, pages)
            gate = jnp.where(blk_lane == idx, neg, gate)
        sel_ref[...] = pages.astype(jnp.int32)


def _sample_select(q, cache_kt, page_table, n_blk, pages_per_step=8):
    db, n_h, n_q, _ = q.shape
    n_pages = page_table.shape[1]
    assert n_pages == LANES and n_blk <= LANES
    ppb = MOBA_BLOCK // PAGE_SIZE
    steps = n_blk * ppb // pages_per_step

    def page_spec(i):
        return pl.BlockSpec((None, n_h, HEAD_DIM, PAGE_SIZE),
                            lambda b, s, pt: (pt[b * n_pages + s * pages_per_step + i], 0, 0, 0))

    grid_spec = pltpu.PrefetchScalarGridSpec(
        num_scalar_prefetch=1,
        grid=(db, steps),
        in_specs=[pl.BlockSpec((None, n_h * n_q, HEAD_DIM), lambda b, s, pt: (b, 0, 0)),
                  pl.BlockSpec((None, 1, n_pages), lambda b, s, pt: (b, 0, 0))]
                 + [page_spec(i) for i in range(pages_per_step)],
        out_specs=pl.BlockSpec((None, n_h * n_q, LANES), lambda b, s, pt: (b, 0, 0)),
        scratch_shapes=[pltpu.VMEM((n_h * HEAD_DIM, LANES), F32)],
    )
    return pl.pallas_call(
        functools.partial(_kmean_select_kernel, pages_per_step=pages_per_step, n_q=n_q, n_blk=n_blk),
        grid_spec=grid_spec,
        out_shape=jax.ShapeDtypeStruct((db, n_h * n_q, LANES), jnp.int32),
        compiler_params=_cparams(2),
        name="sample_select",
    )(page_table.reshape(-1), q.reshape(db, n_h * n_q, HEAD_DIM), page_table.reshape(db, 1, n_pages),
      *([cache_kt] * pages_per_step))


def _sample_attn_kernel(pg_ref, q_ref, kn_ref, vn_ref, *refs, n_q, tiles_per_q):
    n_tiles = n_q * tiles_per_q
    k_refs = refs[:n_tiles]
    v_refs = refs[n_tiles:2 * n_tiles]
    o_ref = refs[2 * n_tiles]
    neg = jnp.float32(-jnp.inf)
    q = q_ref[...]
    kn, vn = kn_ref[...], vn_ref[...]
    rows8 = 8
    q8 = jnp.concatenate([q] * (rows8 // n_q), axis=0)
    kt_cat = jnp.concatenate([r[...] for r in k_refs], axis=1)
    vt_cat = jnp.concatenate([r[...] for r in v_refs], axis=1)
    s = _bdot(q8, kt_cat)
    row = lax.broadcasted_iota(jnp.int32, s.shape, 0) % n_q
    col = lax.broadcasted_iota(jnp.int32, s.shape, 1)
    s = jnp.where((col // (tiles_per_q * PAGE_SIZE)) == row, s, neg)
    row_o = lax.broadcasted_iota(jnp.int32, (rows8, 1), 0) % n_q
    s_own = [jnp.where(row_o >= c, jnp.sum(q8 * kn[c:c + 1, :], axis=-1, keepdims=True), neg)
             for c in range(n_q)]
    m = s.max(axis=-1, keepdims=True)
    for s_c in s_own:
        m = jnp.maximum(m, s_c)
    p = jnp.exp(s - m)
    den = p.sum(axis=-1, keepdims=True)
    acc = _bdot_t(p, vt_cat)
    for c, s_c in enumerate(s_own):
        p_c = jnp.exp(s_c - m)
        den = den + p_c
        acc = acc + p_c * vn[c:c + 1, :]
    o_ref[...] = (acc / den)[:n_q, :]


def _sample_attn(q, k_new, v_new, cache_kt, cache_vt, pages):
    db, n_h, n_q, _ = q.shape
    tiles_per_q = MOBA_TOPK * (MOBA_BLOCK // PAGE_SIZE)
    n_tiles = n_q * tiles_per_q

    def tile_spec(i):
        return pl.BlockSpec((None, None, HEAD_DIM, PAGE_SIZE),
                            lambda b, h, pg: (pg[(b * n_h + h) * n_tiles + i], h, 0, 0))

    small = pl.BlockSpec((None, None, n_q, HEAD_DIM), lambda b, h, pg: (b, h, 0, 0))
    grid_spec = pltpu.PrefetchScalarGridSpec(
        num_scalar_prefetch=1,
        grid=(db, n_h),
        in_specs=[small, small, small] + [tile_spec(i) for i in range(n_tiles)] * 2,
        out_specs=small,
    )
    return pl.pallas_call(
        functools.partial(_sample_attn_kernel, n_q=n_q, tiles_per_q=tiles_per_q),
        grid_spec=grid_spec,
        out_shape=jax.ShapeDtypeStruct((db, n_h, n_q, HEAD_DIM), F32),
        compiler_params=_cparams(2),
        name="sample_attn",
    )(pages, q, k_new, v_new, *([cache_kt] * n_tiles), *([cache_vt] * n_tiles))


def _even_layer(x, conv_prev, gla_prev, p, tm, conv_tt, gla_c, ffn_tm, ffn_tf):
    b, t, d = x.shape
    xf = x.reshape(b * t, d)
    glu, q, k, v, gate, la = _inproj(xf, p["norm_mix"], p["w_in"], p["w_a2"], p["b_a"], tm)
    as_bt = lambda a: a.reshape(b, t, a.shape[-1])
    conv_out, conv_state = _conv_module(as_bt(glu), conv_prev, p["conv_w"], p["conv_b"], p["ln_g"], p["ln_b"], conv_tt)
    o, gla_state = _gla(as_bt(q), as_bt(k), as_bt(la), as_bt(v), as_bt(gate),
                        gla_prev.reshape(b, GLA_HEADS * GLA_DK, GLA_DV), p["gla_norm_g"], gla_c, 2)
    h = _outproj(xf, [conv_out.reshape(b * t, -1), o.reshape(b * t, -1)], p["w_out"], tm)
    h = _ffn(h, p["norm_ffn"], p["ffn_w_gate"], p["ffn_w_up"], p["ffn_w_down"], ffn_tm, ffn_tf)
    return h.reshape(b, t, d), conv_state, gla_state.reshape(b, GLA_HEADS, GLA_DK, GLA_DV)


def kernel(x_prompt, x_sample, state_conv, state_gla, cache_k, cache_v, page_table, norm_mix_even, w_in_even, conv_w, conv_b, conv_ln_g, conv_ln_b, gla_w_a2, gla_b_a, gla_norm_g, w_out_even, norm_ffn_even, ffn_w_gate, ffn_w_up, ffn_w_down, norm_mix_odd, w_qkv_odd, w_out_odd, norm_ffn_odd, router_w, moe_w_gate, moe_w_up, moe_w_down, final_norm_g):
    bp, tp, d = x_prompt.shape
    db, ts, _ = x_sample.shape
    even = dict(norm_mix=norm_mix_even[0], w_in=w_in_even[0], conv_w=conv_w[0], conv_b=conv_b[0],
                ln_g=conv_ln_g[0], ln_b=conv_ln_b[0], w_a2=gla_w_a2[0], b_a=gla_b_a[0],
                gla_norm_g=gla_norm_g[0], w_out=w_out_even[0], norm_ffn=norm_ffn_even[0],
                ffn_w_gate=ffn_w_gate, ffn_w_up=ffn_w_up, ffn_w_down=ffn_w_down)

    hp, conv_p, gla_p = _even_layer(
        x_prompt, jnp.zeros((bp, CONV_W - 1, D_CONV), F32), jnp.zeros((bp, GLA_HEADS, GLA_DK, GLA_DV), F32),
        even, tm=512, conv_tt=256, gla_c=GLA_CHUNK, ffn_tm=1024, ffn_tf=256)

    m_s = db * ts
    xs = x_sample.reshape(m_s, d)
    glu, q, k, v, gate, la = _inproj(xs, even["norm_mix"], even["w_in"], even["w_a2"], even["b_a"], m_s)
    as_bt = lambda a: a.reshape(db, ts, a.shape[-1])
    conv_out_s, conv_s = _conv_module(as_bt(glu), state_conv[0], even["conv_w"], even["conv_b"],
                                      even["ln_g"], even["ln_b"], ts)
    pad_t = 16
    pad8 = lambda a: jnp.pad(as_bt(a), ((0, 0), (0, pad_t - ts), (0, 0)))
    o_s, gla_s = _gla(pad8(q), pad8(k), pad8(la), pad8(v), pad8(gate),
                      state_gla[0].reshape(db, GLA_HEADS * GLA_DK, GLA_DV), even["gla_norm_g"], pad_t, 4)
    hs = _outproj(xs, [conv_out_s.reshape(m_s, -1), o_s[:, :ts].reshape(m_s, -1)], even["w_out"], m_s)
    hs = _ffn(hs, even["norm_ffn"], ffn_w_gate, ffn_w_up, ffn_w_down, m_s, 256)
    gla_s = gla_s.reshape(db, GLA_HEADS, GLA_DK, GLA_DV)

    qp, k_rows, v_rows = _qkv_rope(hp, norm_mix_odd[0], w_qkv_odd[0], jnp.arange(tp), 256, True)
    op = _moba_prompt(qp, k_rows, v_rows)
    moe_wg, moe_wu, moe_wd = moe_w_gate[0].astype(BF16), moe_w_up[0].astype(BF16), moe_w_down[0].astype(BF16)
    hp2, xn_p, gate_p, dest_p = _route(hp.reshape(bp * tp, d), op.reshape(bp * tp, d), w_out_odd[0],
                                       norm_ffn_odd[0], router_w[0], 1024)
    y_prompt = _experts(hp2, xn_p, gate_p, dest_p, moe_wg, moe_wu, moe_wd, final_norm_g,
                        tm=1024, tf=1792, rb=128).reshape(bp, tp, d)

    n_blocks = page_table.shape[1] * PAGE_SIZE // MOBA_BLOCK
    past_len = page_table.shape[1] * PAGE_SIZE
    pos_s = past_len + (jnp.arange(m_s) % ts)
    qs, ks, vs = _qkv_rope(hs.reshape(1, m_s, d), norm_mix_odd[0], w_qkv_odd[0], pos_s, m_s, False)
    to_heads = lambda a: a.reshape(db, ts, ATT_HEADS, HEAD_DIM).transpose(0, 2, 1, 3)
    qs, ks, vs = to_heads(qs), to_heads(ks), to_heads(vs)
    cache_kt, cache_vt = jnp.swapaxes(cache_k[0], -1, -2), jnp.swapaxes(cache_v[0], -1, -2)
    tiles_per_q = MOBA_TOPK * (MOBA_BLOCK // PAGE_SIZE)
    pages = _sample_select(qs, cache_kt, page_table, n_blocks)[:, :, :tiles_per_q].reshape(-1)
    os_ = _sample_attn(qs, ks, vs, cache_kt, cache_vt, pages)
    os_ = os_.transpose(0, 2, 1, 3).reshape(m_s, d)
    hs2 = _outproj(hs, [os_], w_out_odd[0], m_s)
    y_sample = _ffn(hs2, norm_ffn_odd[0], moe_wg, moe_wu, moe_wd, m_s, 512,
                    router_w=router_w[0], final_g=final_norm_g).reshape(db, ts, d)

    return (y_prompt, y_sample, conv_p[None], gla_p[None], k_rows[None], v_rows[None],
            conv_s[None], gla_s[None], ks[None], vs[None])
```

```python
import functools

import jax
import jax.numpy as jnp
import numpy as np
from jax import lax
from jax.experimental import pallas as pl
from jax.experimental.pallas import tpu as pltpu

F32 = jnp.float32
BF16 = jnp.bfloat16
HIGHEST = lax.Precision.HIGHEST

LANES = 128
SUBLANES = 8
NORM_EPS = 1e-6
D_MODEL = 1024
D_CONV = 512
CONV_W = 31
CONV_HALO = 32
GLA_HEADS = 4
GLA_DK = 64
GLA_DV = 128
GLA_TAU = 16.0
GLA_CHUNK = 64
ATT_HEADS = 16
HEAD_DIM = 64
MOBA_BLOCK = 256
MOBA_TOPK = 3
PAGE_SIZE = 128
ROPE_THETA = 10000.0
N_EXPERTS = 8
VMEM_LIMIT = 60 * 1024 * 1024


def _cparams(n_axes):
    return pltpu.CompilerParams(dimension_semantics=("arbitrary",) * n_axes,
                                vmem_limit_bytes=VMEM_LIMIT)


def _rms(x, g):
    return x * lax.rsqrt(jnp.mean(x * x, axis=-1, keepdims=True) + NORM_EPS) * g


def _silu(x):
    return x * jax.nn.sigmoid(x)


def _bdot(a, b):
    return jnp.dot(a.astype(BF16), b.astype(BF16), preferred_element_type=F32)


def _bdot_t(a, b):
    return lax.dot_general(a.astype(BF16), b.astype(BF16), (((1,), (1,)), ((), ())),
                           preferred_element_type=F32)


def _dot_split(a, b):
    a_hi, b_hi = a.astype(BF16), b.astype(BF16)
    a_lo = (a - a_hi.astype(F32)).astype(BF16)
    b_lo = (b - b_hi.astype(F32)).astype(BF16)
    dot = functools.partial(jnp.dot, preferred_element_type=F32)
    return dot(a_hi, b_hi) + (dot(a_hi, b_lo) + dot(a_lo, b_hi))


def _tdot(a, b):
    return lax.dot_general(a.astype(BF16), b.astype(BF16), (((0,), (0,)), ((), ())),
                           preferred_element_type=F32)


def _inproj_kernel(x_ref, g_ref, w_ref, wa1_ref, wa2_ref, ba_ref,
                   glu_ref, q_ref, k_ref, v_ref, gate_ref, la_ref, wbf_ref, wa1bf_ref):
    @pl.when(pl.program_id(0) == 0)
    def _():
        wbf_ref[...] = w_ref[...].astype(BF16)
        wa1bf_ref[...] = wa1_ref[...].astype(BF16)

    xn = _rms(x_ref[...], g_ref[...]).astype(BF16)
    y = jnp.dot(xn, wbf_ref[...], preferred_element_type=F32)
    glu_ref[...] = y[:, :D_CONV] * jax.nn.sigmoid(y[:, D_CONV:2 * D_CONV])
    q_ref[...] = y[:, 1024:1280] * (GLA_DK ** -0.5)
    k_ref[...] = y[:, 1280:1536]
    v_ref[...] = y[:, 1536:2048]
    gate_ref[...] = y[:, 2048:2560]
    a = jnp.dot(xn, wa1bf_ref[...], preferred_element_type=F32)
    z = jnp.dot(a, wa2_ref[...], precision=HIGHEST, preferred_element_type=F32) + ba_ref[...]
    log_sig = jnp.minimum(z, 0.0) - jnp.log(1.0 + jnp.exp(-jnp.abs(z)))
    la_ref[...] = log_sig * (1.0 / GLA_TAU)


def _inproj(x, g, w_in, w_a2, b_a, tm):
    m = x.shape[0]
    n_main = 2560
    wa1 = jnp.pad(w_in[:, n_main:], ((0, 0), (0, LANES - (w_in.shape[1] - n_main))))
    wa2 = jnp.pad(w_a2, ((0, LANES - w_a2.shape[0]), (0, 0)))
    row = lambda i: (i, 0)
    fixed = lambda i: (0, 0)
    outs = [(D_CONV, F32), (256, F32), (256, F32), (512, F32), (512, F32), (256, F32)]
    return pl.pallas_call(
        _inproj_kernel,
        grid=(m // tm,),
        in_specs=[pl.BlockSpec((tm, D_MODEL), row),
                  pl.BlockSpec((1, D_MODEL), fixed),
                  pl.BlockSpec((D_MODEL, n_main), fixed),
                  pl.BlockSpec((D_MODEL, LANES), fixed),
                  pl.BlockSpec((LANES, 256), fixed),
                  pl.BlockSpec((1, 256), fixed)],
        out_specs=[pl.BlockSpec((tm, n), row) for n, _ in outs],
        out_shape=[jax.ShapeDtypeStruct((m, n), dt) for n, dt in outs],
        scratch_shapes=[pltpu.VMEM((D_MODEL, n_main), BF16), pltpu.VMEM((D_MODEL, LANES), BF16)],
        compiler_params=_cparams(1),
        name="inproj",
    )(x, g.reshape(1, -1), w_in, wa1, wa2, b_a.reshape(1, -1))


def _conv_kernel(glu_ref, prev_ref, w_ref, b_ref, lg_ref, lb_ref, out_ref, st_ref, full_ref, ph_ref, *, tt, rc):
    t = pl.program_id(1)
    lo = CONV_HALO - (CONV_W - 1)

    @pl.when(t == 0)
    def _():
        full_ref[0:lo, :] = jnp.zeros((lo, D_CONV), F32)
        full_ref[lo:CONV_HALO, :] = prev_ref[...]

    full_ref[CONV_HALO:CONV_HALO + tt, :] = glu_ref[...]
    span = ph_ref.shape[1]
    for r in range(1, SUBLANES):
        ph_ref[r - 1] = full_ref[r:r + span, :]

    def slab(c, j):
        a, r = divmod(lo + j, SUBLANES)
        start = c * rc + a * SUBLANES
        return full_ref[start:start + rc, :] if r == 0 else ph_ref[r - 1, start:start + rc, :]

    for c in range(tt // rc):
        acc = slab(c, 0) * w_ref[0:1, :]
        for j in range(1, CONV_W):
            acc = acc + slab(c, j) * w_ref[j:j + 1, :]
        y = acc + b_ref[...]
        mu = jnp.mean(y, axis=-1, keepdims=True)
        d = y - mu
        var = jnp.mean(d * d, axis=-1, keepdims=True)
        yn = d * lax.rsqrt(var + NORM_EPS) * lg_ref[...] + lb_ref[...]
        out_ref[c * rc:(c + 1) * rc, :] = _silu(yn).astype(out_ref.dtype)

    tail = full_ref[tt + lo:tt + CONV_HALO, :]

    @pl.when(t == pl.num_programs(1) - 1)
    def _():
        st_ref[...] = tail

    full_ref[lo:CONV_HALO, :] = tail


def _conv_module(glu, prev, conv_w, conv_b, ln_g, ln_b, tt):
    b, t, _ = glu.shape
    rc = min(tt, 32)
    fixed = lambda i, j: (0, 0)
    return pl.pallas_call(
        functools.partial(_conv_kernel, tt=tt, rc=rc),
        grid=(b, t // tt),
        in_specs=[pl.BlockSpec((None, tt, D_CONV), lambda i, j: (i, j, 0)),
                  pl.BlockSpec((None, CONV_W - 1, D_CONV), lambda i, j: (i, 0, 0)),
                  pl.BlockSpec((CONV_W, D_CONV), fixed),
                  pl.BlockSpec((1, D_CONV), fixed),
                  pl.BlockSpec((1, D_CONV), fixed),
                  pl.BlockSpec((1, D_CONV), fixed)],
        out_specs=[pl.BlockSpec((None, tt, D_CONV), lambda i, j: (i, j, 0)),
                   pl.BlockSpec((None, CONV_W - 1, D_CONV), lambda i, j: (i, 0, 0))],
        out_shape=[jax.ShapeDtypeStruct((b, t, D_CONV), BF16),
                   jax.ShapeDtypeStruct((b, CONV_W - 1, D_CONV), F32)],
        scratch_shapes=[pltpu.VMEM((CONV_HALO + tt, D_CONV), F32),
                        pltpu.VMEM((SUBLANES - 1, CONV_HALO + tt - SUBLANES, D_CONV), F32)],
        compiler_params=_cparams(2),
        name="conv_module",
    )(glu, prev, conv_w, conv_b.reshape(1, -1), ln_g.reshape(1, -1), ln_b.reshape(1, -1))


def _gla_levels(c):
    levels, g_big = [], c
    while g_big > 1:
        g_small = max(g_big // 4, 1)
        levels.append((g_big, g_small))
        g_big = g_small
    return levels


def _gla_decay_matrices(c):
    t = np.arange(c)[:, None]
    s = np.arange(c)[None, :]
    mats = [(s <= t), (s > t)]
    for g_big, g_small in _gla_levels(c):
        sub = (t % g_big) // g_small
        ref_q = (t // g_big) * g_big + sub * g_small - 1
        mats.append((sub >= 1) & (s > ref_q) & (s <= t))
        for i in range(1, g_big // g_small):
            ref_k = (t // g_big) * g_big + i * g_small - 1
            mats.append(((t % g_big) < i * g_small) & (s > t) & (s <= ref_k))
    return np.concatenate([m.astype(np.float32) for m in mats], axis=0)


def _gla_kernel(q_ref, k_ref, la_ref, v_ref, gate_ref, s0_ref, ng_ref, dmat_ref,
                out_ref, st_ref, s_ref, *, c):
    ci = pl.program_id(1)

    @pl.when(ci == 0)
    def _():
        s_ref[...] = s0_ref[...]

    for bi in range(q_ref.shape[0]):
        out, new_state = _gla_chunk(q_ref[bi], k_ref[bi], la_ref[bi], v_ref[bi], gate_ref[bi], s_ref[bi],
                                    ng_ref[...], dmat_ref[...], c)
        out_ref[bi] = out.astype(out_ref.dtype)
        s_ref[bi] = new_state

    @pl.when(ci == pl.num_programs(1) - 1)
    def _():
        st_ref[...] = s_ref[...]


def _gla_chunk(q, k, la, v, gate, state, norm_g, dmat, c):
    hk = GLA_HEADS * GLA_DK
    la1 = la.astype(BF16)
    la2 = (la - la1.astype(F32)).astype(BF16)
    e_all = (jnp.dot(dmat, la1, preferred_element_type=F32)
             + jnp.dot(dmat, la2, preferred_element_type=F32))

    def e_blk(i):
        return e_all[i * c:(i + 1) * c, :]

    lane_head = lax.broadcasted_iota(jnp.int32, (c, hk), 1) // GLA_DK
    head_masks = [(lane_head == h).astype(F32) for h in range(GLA_HEADS)]

    def stack_heads(x):
        return jnp.concatenate([x * m for m in head_masks], axis=0).astype(BF16)

    row_t = lax.broadcasted_iota(jnp.int32, (GLA_HEADS * c, c), 0) % c
    col_s = lax.broadcasted_iota(jnp.int32, (GLA_HEADS * c, c), 1)
    row_k = lax.broadcasted_iota(jnp.int32, (c, hk), 0)

    o_st = jnp.dot(stack_heads(q * jnp.exp(e_blk(0))), state.astype(BF16),
                   preferred_element_type=F32)

    att = jnp.where(row_t == col_s, _bdot_t(stack_heads(q), k), 0.0)
    idx = 2
    for g_big, g_small in _gla_levels(c):
        q_lvl = stack_heads(q * jnp.exp(e_blk(idx)))
        idx += 1
        n_var = g_big // g_small - 1
        k_vars = []
        for i in range(1, n_var + 1):
            k_vars.append(jnp.where((row_k % g_big) < i * g_small, k * jnp.exp(e_blk(idx)), 0.0))
            idx += 1
        p = _bdot_t(q_lvl, jnp.concatenate(k_vars, axis=0))
        wide_t = lax.broadcasted_iota(jnp.int32, p.shape, 0) % c
        wide_col = lax.broadcasted_iota(jnp.int32, p.shape, 1)
        keep = ((wide_t // g_big) == ((wide_col % c) // g_big)) & ((wide_t % g_big) // g_small == wide_col // c + 1)
        p = jnp.where(keep, p, 0.0)
        for i in range(n_var):
            att = att + p[:, i * c:(i + 1) * c]

    att = att.astype(BF16)
    outs = []
    for h in range(GLA_HEADS):
        v_h = v[:, h * GLA_DV:(h + 1) * GLA_DV]
        o_h = o_st[h * c:(h + 1) * c, :] + jnp.dot(att[h * c:(h + 1) * c, :], v_h.astype(BF16),
                                                   preferred_element_type=F32)
        o_h = _rms(o_h, norm_g[:, h * GLA_DV:(h + 1) * GLA_DV])
        outs.append(o_h * _silu(gate[:, h * GLA_DV:(h + 1) * GLA_DV]))

    kd = (k * jnp.exp(e_blk(1))).astype(BF16)
    ones = jnp.ones((c, GLA_DV), BF16)
    b_last = _tdot(la1, ones) + _tdot(la2, ones)
    row_head = lax.broadcasted_iota(jnp.int32, (hk, GLA_DV), 0) // GLA_DK
    upd = jnp.zeros((hk, GLA_DV), F32)
    for h in range(GLA_HEADS):
        kv = _tdot(kd, v[:, h * GLA_DV:(h + 1) * GLA_DV])
        upd = upd + jnp.where(row_head == h, kv, 0.0)
    return jnp.concatenate(outs, axis=1), jnp.exp(b_last) * state + upd


def _gla(q, k, la, v, gate, s0, norm_g, c, nbs):
    b, t, hk = q.shape
    assert b % nbs == 0 and t % c == 0
    dv_all = v.shape[-1]
    dmat = jnp.asarray(_gla_decay_matrices(c), dtype=BF16)
    tok = lambda n: pl.BlockSpec((nbs, c, n), lambda i, j: (i, j, 0))
    fixed = lambda i, j: (0, 0)
    return pl.pallas_call(
        functools.partial(_gla_kernel, c=c),
        grid=(b // nbs, t // c),
        in_specs=[tok(hk), tok(hk), tok(hk), tok(dv_all), tok(dv_all),
                  pl.BlockSpec((nbs, hk, GLA_DV), lambda i, j: (i, 0, 0)),
                  pl.BlockSpec((1, dv_all), fixed),
                  pl.BlockSpec(dmat.shape, fixed)],
        out_specs=[tok(dv_all), pl.BlockSpec((nbs, hk, GLA_DV), lambda i, j: (i, 0, 0))],
        out_shape=[jax.ShapeDtypeStruct((b, t, dv_all), BF16),
                   jax.ShapeDtypeStruct((b, hk, GLA_DV), F32)],
        scratch_shapes=[pltpu.VMEM((nbs, hk, GLA_DV), F32)],
        compiler_params=_cparams(2),
        name="gla",
    )(q, k, la, v, gate, s0, norm_g.reshape(1, -1), dmat)


def _outproj_kernel(*refs, n_parts):
    x_ref = refs[0]
    a_refs = refs[1:1 + n_parts]
    w_ref = refs[1 + n_parts]
    out_ref = refs[2 + n_parts]
    wbf_ref = refs[3 + n_parts]

    @pl.when(pl.program_id(0) == 0)
    def _():
        wbf_ref[...] = w_ref[...].astype(BF16)

    acc = x_ref[...]
    off = 0
    for a_ref in a_refs:
        n = a_ref.shape[-1]
        acc = acc + jnp.dot(a_ref[...].astype(BF16), wbf_ref[off:off + n, :], preferred_element_type=F32)
        off += n
    out_ref[...] = acc


def _outproj(x, parts, w, tm):
    m = x.shape[0]
    row = lambda i: (i, 0)
    return pl.pallas_call(
        functools.partial(_outproj_kernel, n_parts=len(parts)),
        grid=(m // tm,),
        in_specs=[pl.BlockSpec((tm, D_MODEL), row)]
                 + [pl.BlockSpec((tm, p.shape[-1]), row) for p in parts]
                 + [pl.BlockSpec(w.shape, lambda i: (0, 0))],
        out_specs=pl.BlockSpec((tm, D_MODEL), row),
        out_shape=jax.ShapeDtypeStruct((m, D_MODEL), F32),
        scratch_shapes=[pltpu.VMEM(w.shape, BF16)],
        compiler_params=_cparams(1),
        name="outproj",
    )(x, *parts, w)


def _ffn_kernel(*refs, routed, final_norm):
    it = iter(refs)
    h_ref, g_ref = next(it), next(it)
    rw_ref = next(it) if routed else None
    wg_ref, wu_ref, wd_ref = next(it), next(it), next(it)
    fg_ref = next(it) if final_norm else None
    out_ref, xn_ref, acc_ref = next(it), next(it), next(it)
    gate_ref = next(it) if routed else None
    e, f = pl.program_id(1), pl.program_id(2)

    @pl.when((e == 0) & (f == 0))
    def _():
        xn = _rms(h_ref[...], g_ref[...])
        xn_ref[...] = xn.astype(BF16)
        acc_ref[...] = jnp.zeros_like(acc_ref)
        if routed:
            gate_ref[...] = _top2_gates(jnp.dot(xn, rw_ref[...], precision=HIGHEST, preferred_element_type=F32))

    xn = xn_ref[...]
    hid = _silu(jnp.dot(xn, wg_ref[...].astype(BF16), preferred_element_type=F32)) \
        * jnp.dot(xn, wu_ref[...].astype(BF16), preferred_element_type=F32)
    if routed:
        lane = lax.broadcasted_iota(jnp.int32, gate_ref.shape, 1)
        hid = hid * jnp.sum(jnp.where(lane == e, gate_ref[...], 0.0), axis=-1, keepdims=True)
    acc_ref[...] += jnp.dot(hid.astype(BF16), wd_ref[...].astype(BF16), preferred_element_type=F32)

    @pl.when((e == pl.num_programs(1) - 1) & (f == pl.num_programs(2) - 1))
    def _():
        y = h_ref[...] + acc_ref[...]
        if final_norm:
            y = _rms(y, fg_ref[...])
        out_ref[...] = y


def _ffn(h, g, w_gate, w_up, w_down, tm, tf, router_w=None, final_g=None):
    m = h.shape[0]
    n_e, _, d_ff = w_gate.shape
    routed = router_w is not None
    final_norm = final_g is not None
    row = lambda i, e, f: (i, 0)
    fixed = lambda i, e, f: (0, 0)
    args = [h, g.reshape(1, -1)]
    in_specs = [pl.BlockSpec((tm, D_MODEL), row), pl.BlockSpec((1, D_MODEL), fixed)]
    if routed:
        args.append(jnp.pad(router_w, ((0, 0), (0, LANES - router_w.shape[1]))))
        in_specs.append(pl.BlockSpec((D_MODEL, LANES), fixed))
    args += [w_gate, w_up, w_down]
    in_specs += [pl.BlockSpec((None, D_MODEL, tf), lambda i, e, f: (e, 0, f)),
                 pl.BlockSpec((None, D_MODEL, tf), lambda i, e, f: (e, 0, f)),
                 pl.BlockSpec((None, tf, D_MODEL), lambda i, e, f: (e, f, 0))]
    if final_norm:
        args.append(final_g.reshape(1, -1))
        in_specs.append(pl.BlockSpec((1, D_MODEL), fixed))
    scratch = [pltpu.VMEM((tm, D_MODEL), BF16), pltpu.VMEM((tm, D_MODEL), F32)]
    if routed:
        scratch.append(pltpu.VMEM((tm, LANES), F32))
    return pl.pallas_call(
        functools.partial(_ffn_kernel, routed=routed, final_norm=final_norm),
        grid=(m // tm, n_e, d_ff // tf),
        in_specs=in_specs,
        out_specs=pl.BlockSpec((tm, D_MODEL), row),
        out_shape=jax.ShapeDtypeStruct((m, D_MODEL), F32),
        scratch_shapes=scratch,
        compiler_params=_cparams(3),
        name="moe_ffn" if routed else "ffn",
    )(*args)


def _top2_gates(logits):
    lane = lax.broadcasted_iota(jnp.int32, logits.shape, 1)
    neg = jnp.float32(-jnp.inf)
    logits = jnp.where(lane < N_EXPERTS, logits, neg)
    m1 = jnp.max(logits, axis=-1, keepdims=True)
    i1 = jnp.min(jnp.where(logits == m1, lane, LANES), axis=-1, keepdims=True)
    rest = jnp.where(lane == i1, neg, logits)
    m2 = jnp.max(rest, axis=-1, keepdims=True)
    i2 = jnp.min(jnp.where(rest == m2, lane, LANES), axis=-1, keepdims=True)
    e2 = jnp.exp(m2 - m1)
    g1 = 1.0 / (1.0 + e2)
    g2 = e2 / (1.0 + e2)
    return jnp.where(lane == i1, g1, 0.0) + jnp.where(lane == i2, g2, 0.0)


def _route_kernel(x_ref, a_ref, w_ref, g_ref, rw_ref, h_ref, xn_ref, gate_ref, dest_ref, wbf_ref, ltri_ref):
    tm = x_ref.shape[0]

    @pl.when(pl.program_id(0) == 0)
    def _():
        wbf_ref[...] = w_ref[...].astype(BF16)
        earlier = (lax.broadcasted_iota(jnp.int32, (tm, tm), 1) < lax.broadcasted_iota(jnp.int32, (tm, tm), 0))
        ltri_ref[...] = jnp.where(earlier, 1.0, 0.0).astype(BF16)

    h = x_ref[...] + jnp.dot(a_ref[...].astype(BF16), wbf_ref[...], preferred_element_type=F32)
    h_ref[...] = h
    xn = _rms(h, g_ref[...])
    xn_ref[...] = xn.astype(BF16)
    gate = _top2_gates(_dot_split(xn, rw_ref[...]))
    gate_ref[...] = gate
    routed = gate > 0.0
    rank = jnp.dot(ltri_ref[...], jnp.where(routed, 1.0, 0.0).astype(BF16), preferred_element_type=F32)
    dest_ref[...] = jnp.where(routed, rank, -1.0)


def _route(x, a, w_out, g, router_w, tm):
    m = x.shape[0]
    row = lambda i: (i, 0)
    fixed = lambda i: (0, 0)
    rw = jnp.pad(router_w, ((0, 0), (0, LANES - router_w.shape[1])))
    return pl.pallas_call(
        _route_kernel,
        grid=(m // tm,),
        in_specs=[pl.BlockSpec((tm, D_MODEL), row), pl.BlockSpec((tm, D_MODEL), row),
                  pl.BlockSpec(w_out.shape, fixed), pl.BlockSpec((1, D_MODEL), fixed),
                  pl.BlockSpec((D_MODEL, LANES), fixed)],
        out_specs=[pl.BlockSpec((tm, D_MODEL), row), pl.BlockSpec((tm, D_MODEL), row),
                   pl.BlockSpec((tm, LANES), row), pl.BlockSpec((tm, LANES), row)],
        out_shape=[jax.ShapeDtypeStruct((m, D_MODEL), F32), jax.ShapeDtypeStruct((m, D_MODEL), BF16),
                   jax.ShapeDtypeStruct((m, LANES), F32), jax.ShapeDtypeStruct((m, LANES), F32)],
        scratch_shapes=[pltpu.VMEM(w_out.shape, BF16), pltpu.VMEM((tm, tm), BF16)],
        compiler_params=_cparams(1),
        name="route",
    )(x, a, w_out, g.reshape(1, -1), rw)


def _experts_kernel(cnt_ref, xn_ref, gate_ref, dest_ref, h_ref, wg_ref, wu_ref, wd_ref, fg_ref, out_ref,
                    dest_t_ref, xc_ref, yc_ref, *, rb):
    i, e, f = pl.program_id(0), pl.program_id(1), pl.program_id(2)
    n_e, n_f = pl.num_programs(1), pl.num_programs(2)
    tm = xn_ref.shape[0]

    @pl.when((e == 0) & (f == 0))
    def _():
        dest_t_ref[...] = jnp.transpose(dest_ref[...])
        out_ref[...] = h_ref[...]

    n_blk = (cnt_ref[i * n_e + e] + rb - 1) // rb
    slot_row = dest_t_ref[pl.ds(e, 1), :]
    lane = lax.broadcasted_iota(jnp.int32, (tm, LANES), 1)
    slot_col = jnp.sum(jnp.where(lane == e, dest_ref[...], 0.0), axis=-1, keepdims=True)

    def rows(blk):
        return pl.ds(pl.multiple_of(blk * rb, rb), rb)

    mb = 2 * rb
    n_mv = (n_blk + 1) // 2

    def mrows(blk):
        return pl.ds(pl.multiple_of(blk * mb, mb), mb)

    @pl.when(f == 0)
    def _():
        def gather(blk, carry):
            want = (blk * mb + lax.broadcasted_iota(jnp.int32, (mb, tm), 0)).astype(F32)
            pick = jnp.where(slot_row == want, 1.0, 0.0).astype(BF16)
            xc_ref[mrows(blk), :] = jnp.dot(pick, xn_ref[...], preferred_element_type=F32).astype(BF16)
            yc_ref[mrows(blk), :] = jnp.zeros((mb, D_MODEL), F32)
            return carry
        lax.fori_loop(0, n_mv, gather, 0)

    def expert(blk, carry):
        xc = xc_ref[rows(blk), :]
        hid = _silu(jnp.dot(xc, wg_ref[...], preferred_element_type=F32)) \
            * jnp.dot(xc, wu_ref[...], preferred_element_type=F32)
        yc_ref[rows(blk), :] += jnp.dot(hid.astype(BF16), wd_ref[...], preferred_element_type=F32)
        return carry
    lax.fori_loop(0, n_blk, expert, 0)

    @pl.when(f == n_f - 1)
    def _():
        gate_col = jnp.sum(jnp.where(lane == e, gate_ref[...], 0.0), axis=-1, keepdims=True)

        def scatter(blk, carry):
            want = (blk * mb + lax.broadcasted_iota(jnp.int32, (tm, mb), 1)).astype(F32)
            place = jnp.where(slot_col == want, 1.0, 0.0).astype(BF16)
            out_ref[...] += gate_col * jnp.dot(place, yc_ref[mrows(blk), :].astype(BF16),
                                               preferred_element_type=F32)
            return carry
        lax.fori_loop(0, n_mv, scatter, 0)

    @pl.when((e == n_e - 1) & (f == n_f - 1))
    def _():
        out_ref[...] = _rms(out_ref[...], fg_ref[...])


def _experts(h, xn, gate, dest, w_gate, w_up, w_down, final_g, tm, tf, rb):
    m = h.shape[0]
    n_e, _, d_ff = w_gate.shape
    assert tm % (2 * rb) == 0 and m % tm == 0 and d_ff % tf == 0
    counts = jnp.sum((dest[:, :n_e] >= 0.0).reshape(m // tm, tm, n_e), axis=1, dtype=jnp.int32).reshape(-1)
    row = lambda i, e, f, c: (i, 0)
    grid_spec = pltpu.PrefetchScalarGridSpec(
        num_scalar_prefetch=1,
        grid=(m // tm, n_e, d_ff // tf),
        in_specs=[pl.BlockSpec((tm, D_MODEL), row), pl.BlockSpec((tm, LANES), row), pl.BlockSpec((tm, LANES), row),
                  pl.BlockSpec((tm, D_MODEL), row),
                  pl.BlockSpec((None, D_MODEL, tf), lambda i, e, f, c: (e, 0, f)),
                  pl.BlockSpec((None, D_MODEL, tf), lambda i, e, f, c: (e, 0, f)),
                  pl.BlockSpec((None, tf, D_MODEL), lambda i, e, f, c: (e, f, 0)),
                  pl.BlockSpec((1, D_MODEL), lambda i, e, f, c: (0, 0))],
        out_specs=pl.BlockSpec((tm, D_MODEL), row),
        scratch_shapes=[pltpu.VMEM((LANES, tm), F32), pltpu.VMEM((tm, D_MODEL), BF16),
                        pltpu.VMEM((tm, D_MODEL), F32)],
    )
    return pl.pallas_call(
        functools.partial(_experts_kernel, rb=rb),
        grid_spec=grid_spec,
        out_shape=jax.ShapeDtypeStruct((m, D_MODEL), F32),
        compiler_params=_cparams(3),
        name="experts",
    )(counts, xn, gate, dest, h, w_gate, w_up, w_down, final_g.reshape(1, -1))


def _rope_tables(pos):
    half = HEAD_DIM // 2
    inv = jnp.power(ROPE_THETA, -jnp.arange(half, dtype=F32) / half)
    ang = pos.astype(F32)[:, None] * inv[None, :]
    cos, sin = jnp.cos(ang), jnp.sin(ang)
    cos_t = jnp.concatenate([cos, cos, cos, cos], axis=1)
    sin_t = jnp.concatenate([-sin, sin, -sin, sin], axis=1)
    return cos_t, sin_t


def _rope_apply(y, cos_t, sin_t):
    half = HEAD_DIM // 2
    first = (lax.broadcasted_iota(jnp.int32, (y.shape[0], LANES), 1) % HEAD_DIM) < half
    cols = []
    for c in range(y.shape[1] // LANES):
        yc = y[:, c * LANES:(c + 1) * LANES]
        swapped = jnp.where(first, pltpu.roll(yc, LANES - half, 1), pltpu.roll(yc, half, 1))
        cols.append(yc * cos_t + swapped * sin_t)
    return jnp.concatenate(cols, axis=1)


def _qkv_kernel(x_ref, g_ref, w_ref, cos_ref, sin_ref, q_ref, k_ref, v_ref, wbf_ref, *, head_major):
    @pl.when((pl.program_id(0) == 0) & (pl.program_id(1) == 0))
    def _():
        wbf_ref[...] = w_ref[...].astype(BF16)

    xn = _rms(x_ref[...], g_ref[...]).astype(BF16)
    y = jnp.dot(xn, wbf_ref[...], preferred_element_type=F32)
    cos_t, sin_t = cos_ref[...], sin_ref[...]
    q = _rope_apply(y[:, :D_MODEL], cos_t, sin_t) * (HEAD_DIM ** -0.5)
    k = _rope_apply(y[:, D_MODEL:2 * D_MODEL], cos_t, sin_t)
    v = y[:, 2 * D_MODEL:]
    if head_major:
        tm = q.shape[0]
        for h in range(ATT_HEADS):
            sl = slice(h * HEAD_DIM, (h + 1) * HEAD_DIM)
            q_ref[h] = q[:, sl]
            for p in range(tm // PAGE_SIZE):
                rows = slice(p * PAGE_SIZE, (p + 1) * PAGE_SIZE)
                k_ref[p, h] = k[rows, sl]
                v_ref[p, h] = v[rows, sl]
    else:
        q_ref[...] = q
        k_ref[...] = k
        v_ref[...] = v


def _qkv_rope(x, g, w_qkv, pos, tm, head_major):
    b, t, _ = x.shape
    cos_t, sin_t = _rope_tables(pos)
    fixed = lambda i, j: (0, 0)
    if head_major:
        npg = tm // PAGE_SIZE
        out_specs = [pl.BlockSpec((None, ATT_HEADS, tm, HEAD_DIM), lambda i, j: (i, 0, j, 0)),
                     pl.BlockSpec((None, npg, ATT_HEADS, PAGE_SIZE, HEAD_DIM), lambda i, j: (i, j, 0, 0, 0)),
                     pl.BlockSpec((None, npg, ATT_HEADS, PAGE_SIZE, HEAD_DIM), lambda i, j: (i, j, 0, 0, 0))]
        kv_shape = (b, t // PAGE_SIZE, ATT_HEADS, PAGE_SIZE, HEAD_DIM)
        out_shape = [jax.ShapeDtypeStruct((b, ATT_HEADS, t, HEAD_DIM), F32),
                     jax.ShapeDtypeStruct(kv_shape, F32), jax.ShapeDtypeStruct(kv_shape, F32)]
    else:
        out_specs = [pl.BlockSpec((None, tm, D_MODEL), lambda i, j: (i, j, 0))] * 3
        out_shape = [jax.ShapeDtypeStruct((b, t, D_MODEL), F32)] * 3
    return pl.pallas_call(
        functools.partial(_qkv_kernel, head_major=head_major),
        grid=(b, t // tm),
        in_specs=[pl.BlockSpec((None, tm, D_MODEL), lambda i, j: (i, j, 0)),
                  pl.BlockSpec((1, D_MODEL), fixed),
                  pl.BlockSpec(w_qkv.shape, fixed),
                  pl.BlockSpec((tm, LANES), lambda i, j: (j, 0)),
                  pl.BlockSpec((tm, LANES), lambda i, j: (j, 0))],
        out_specs=out_specs,
        out_shape=out_shape,
        scratch_shapes=[pltpu.VMEM(w_qkv.shape, BF16)],
        compiler_params=_cparams(2),
        name="qkv_rope",
    )(x, g.reshape(1, -1), w_qkv, cos_t, sin_t)


def _top_blocks(gate, valid, n_keep):
    nb = gate.shape[0]
    blk = lax.broadcasted_iota(jnp.int32, gate.shape, 0)
    rank = jnp.zeros(gate.shape, F32)
    for m in range(nb):
        g_m = gate[m:m + 1, :]
        beats = (g_m > gate) | ((g_m == gate) & (m < blk))
        rank = rank + jnp.where(beats & valid[m:m + 1, :], 1.0, 0.0)
    return jnp.where(valid & (rank < n_keep), 1.0, 0.0)


def _moba_prompt_kernel(q_ref, k_ref, v_ref, o_ref, *, t, heads_per_step):
    nb = t // MOBA_BLOCK
    neg = jnp.float32(-jnp.inf)
    outs = []
    for hh in range(heads_per_step):
        q = q_ref[hh]
        k = k_ref[:, hh].reshape(t, HEAD_DIM)
        v = v_ref[:, hh].reshape(t, HEAD_DIM)
        kmean = jnp.concatenate(
            [jnp.mean(k[n * MOBA_BLOCK:(n + 1) * MOBA_BLOCK, :], axis=0, keepdims=True) for n in range(nb)], axis=0)
        gate = lax.dot_general(kmean, q, (((1,), (1,)), ((), ())), precision=HIGHEST,
                               preferred_element_type=F32)
        own_of_q = lax.broadcasted_iota(jnp.int32, (nb, t), 1) // MOBA_BLOCK
        valid = lax.broadcasted_iota(jnp.int32, (nb, t), 0) < own_of_q
        sel_t = _top_blocks(gate, valid, MOBA_TOPK)
        pad_rows = 16 - nb
        sel_t = jnp.concatenate([sel_t, jnp.zeros((pad_rows, t), F32)], axis=0)
        eye = jnp.where(lax.broadcasted_iota(jnp.int32, (16, LANES), 0)
                        == lax.broadcasted_iota(jnp.int32, (16, LANES), 1), 1.0, 0.0)
        sel = _tdot(sel_t, eye)
        kb = k.astype(BF16)
        vb = jnp.concatenate([v, jnp.ones((t, LANES - HEAD_DIM), F32)], axis=1).astype(BF16)
        causal = (lax.broadcasted_iota(jnp.int32, (MOBA_BLOCK, MOBA_BLOCK), 1)
                  <= lax.broadcasted_iota(jnp.int32, (MOBA_BLOCK, MOBA_BLOCK), 0))
        o_blocks = []
        for own in range(nb):
            rows = slice(own * MOBA_BLOCK, (own + 1) * MOBA_BLOCK)
            qb = q[rows, :].astype(BF16)
            n_keys = (own + 1) * MOBA_BLOCK
            s_all = _bdot_t(qb, kb[:n_keys, :])
            scores = []
            for n in range(own):
                s = s_all[:, n * MOBA_BLOCK:(n + 1) * MOBA_BLOCK]
                scores.append(jnp.where(sel[rows, n:n + 1] > 0.0, s, neg))
            scores.append(jnp.where(causal, s_all[:, own * MOBA_BLOCK:], neg))
            m_elem = scores[0]
            for s in scores[1:]:
                m_elem = jnp.maximum(m_elem, s)
            m = m_elem.max(axis=-1, keepdims=True)
            p_all = jnp.concatenate([jnp.exp(s - m).astype(BF16) for s in scores], axis=1)
            acc = jnp.dot(p_all, vb[:n_keys, :], preferred_element_type=F32)
            o_blocks.append(acc[:, :HEAD_DIM] / acc[:, HEAD_DIM:HEAD_DIM + 1])
        outs.append(jnp.concatenate(o_blocks, axis=0))
    o_ref[...] = jnp.concatenate(outs, axis=1)


def _moba_prompt(q, k_rows, v_rows):
    b, n_h, t, _ = q.shape
    hps = LANES // HEAD_DIM
    npg = t // PAGE_SIZE
    kv_spec = pl.BlockSpec((None, npg, hps, PAGE_SIZE, HEAD_DIM), lambda i, j: (i, 0, j, 0, 0))
    return pl.pallas_call(
        functools.partial(_moba_prompt_kernel, t=t, heads_per_step=hps),
        grid=(b, n_h // hps),
        in_specs=[pl.BlockSpec((None, hps, t, HEAD_DIM), lambda i, j: (i, j, 0, 0)), kv_spec, kv_spec],
        out_specs=pl.BlockSpec((None, t, LANES), lambda i, j: (i, 0, j)),
        out_shape=jax.ShapeDtypeStruct((b, t, n_h * HEAD_DIM), F32),
        compiler_params=_cparams(2),
        name="moba_prompt",
    )(q, k_rows, v_rows)


def _kmean_select_kernel(pt_ref, q_ref, ptv_ref, *refs, pages_per_step, n_q, n_blk):
    page_refs = refs[:pages_per_step]
    sel_ref = refs[pages_per_step]
    kmt_ref = refs[pages_per_step + 1]
    s = pl.program_id(1)
    ppb = MOBA_BLOCK // PAGE_SIZE
    rows = ATT_HEADS * HEAD_DIM

    @pl.when(s == 0)
    def _():
        kmt_ref[...] = jnp.zeros_like(kmt_ref)

    lane = lax.broadcasted_iota(jnp.int32, (rows, LANES), 1)
    kmt = kmt_ref[...]
    for blk in range(pages_per_step // ppb):
        tot = page_refs[blk * ppb][...]
        for i in range(1, ppb):
            tot = tot + page_refs[blk * ppb + i][...]
        mean = tot.reshape(rows, PAGE_SIZE).sum(axis=-1, keepdims=True) * (1.0 / MOBA_BLOCK)
        kmt = jnp.where(lane == s * (pages_per_step // ppb) + blk, mean, kmt)
    kmt_ref[...] = kmt

    @pl.when(s == pl.num_programs(1) - 1)
    def _():
        q = q_ref[...]
        n_rows = q.shape[0]
        row_head = lax.broadcasted_iota(jnp.int32, (n_rows, LANES), 0) // n_q
        gate = jnp.zeros((n_rows, LANES), F32)
        for h in range(ATT_HEADS):
            g_h = jnp.dot(q, kmt[h * HEAD_DIM:(h + 1) * HEAD_DIM, :], precision=HIGHEST,
                          preferred_element_type=F32)
            gate = jnp.where(row_head == h, g_h, gate)
        blk_lane = lax.broadcasted_iota(jnp.int32, gate.shape, 1)
        neg = jnp.float32(-jnp.inf)
        gate = jnp.where(blk_lane < n_blk, gate, neg)
        pt_row = ptv_ref[...].astype(F32)
        pages = jnp.zeros(gate.shape, F32)
        for r in range(MOBA_TOPK):
            m = jnp.max(gate, axis=-1, keepdims=True)
            idx = jnp.min(jnp.where(gate == m, blk_lane, LANES), axis=-1, keepdims=True)
            for i in range(ppb):
                page = jnp.sum(jnp.where(blk_lane == idx * ppb + i, pt_row, 0.0), axis=-1, keepdims=True)
                pages = jnp.where(blk_lane == r * ppb + i, page, pages)
            gate = jnp.where(blk_lane == idx, neg, gate)
        sel_ref[...] = pages.astype(jnp.int32)


def _sample_select(q, cache_kt, page_table, n_blk, pages_per_step=16):
    db, n_h, n_q, _ = q.shape
    n_pages = page_table.shape[1]
    assert n_pages == LANES and n_blk <= LANES
    ppb = MOBA_BLOCK // PAGE_SIZE
    steps = n_blk * ppb // pages_per_step

    def page_spec(i):
        return pl.BlockSpec((None, n_h, HEAD_DIM, PAGE_SIZE),
                            lambda b, s, pt: (pt[b * n_pages + s * pages_per_step + i], 0, 0, 0))

    grid_spec = pltpu.PrefetchScalarGridSpec(
        num_scalar_prefetch=1,
        grid=(db, steps),
        in_specs=[pl.BlockSpec((None, n_h * n_q, HEAD_DIM), lambda b, s, pt: (b, 0, 0)),
                  pl.BlockSpec((None, 1, n_pages), lambda b, s, pt: (b, 0, 0))]
                 + [page_spec(i) for i in range(pages_per_step)],
        out_specs=pl.BlockSpec((None, n_h * n_q, LANES), lambda b, s, pt: (b, 0, 0)),
        scratch_shapes=[pltpu.VMEM((n_h * HEAD_DIM, LANES), F32)],
    )
    return pl.pallas_call(
        functools.partial(_kmean_select_kernel, pages_per_step=pages_per_step, n_q=n_q, n_blk=n_blk),
        grid_spec=grid_spec,
        out_shape=jax.ShapeDtypeStruct((db, n_h * n_q, LANES), jnp.int32),
        compiler_params=_cparams(2),
        name="sample_select",
    )(page_table.reshape(-1), q.reshape(db, n_h * n_q, HEAD_DIM), page_table.reshape(db, 1, n_pages),
      *([cache_kt] * pages_per_step))


def _sample_attn_kernel(pg_ref, q_ref, kn_ref, vn_ref, *refs, n_q, tiles_per_q):
    n_tiles = n_q * tiles_per_q
    k_refs = refs[:n_tiles]
    v_refs = refs[n_tiles:2 * n_tiles]
    o_ref = refs[2 * n_tiles]
    neg = jnp.float32(-jnp.inf)
    q = q_ref[...]
    kn, vn = kn_ref[...], vn_ref[...]
    rows8 = 8
    q8 = jnp.concatenate([q] * (rows8 // n_q), axis=0)
    kt_cat = jnp.concatenate([r[...] for r in k_refs], axis=1)
    vt_cat = jnp.concatenate([r[...] for r in v_refs], axis=1)
    s = _bdot(q8, kt_cat)
    row = lax.broadcasted_iota(jnp.int32, s.shape, 0) % n_q
    col = lax.broadcasted_iota(jnp.int32, s.shape, 1)
    s = jnp.where((col // (tiles_per_q * PAGE_SIZE)) == row, s, neg)
    row_o = lax.broadcasted_iota(jnp.int32, (rows8, 1), 0) % n_q
    s_own = [jnp.where(row_o >= c, jnp.sum(q8 * kn[c:c + 1, :], axis=-1, keepdims=True), neg)
             for c in range(n_q)]
    m = s.max(axis=-1, keepdims=True)
    for s_c in s_own:
        m = jnp.maximum(m, s_c)
    p = jnp.exp(s - m)
    den = p.sum(axis=-1, keepdims=True)
    acc = _bdot_t(p, vt_cat)
    for c, s_c in enumerate(s_own):
        p_c = jnp.exp(s_c - m)
        den = den + p_c
        acc = acc + p_c * vn[c:c + 1, :]
    o_ref[...] = (acc / den)[:n_q, :]


def _sample_attn(q, k_new, v_new, cache_kt, cache_vt, pages):
    db, n_h, n_q, _ = q.shape
    tiles_per_q = MOBA_TOPK * (MOBA_BLOCK // PAGE_SIZE)
    n_tiles = n_q * tiles_per_q

    def tile_spec(i):
        return pl.BlockSpec((None, None, HEAD_DIM, PAGE_SIZE),
                            lambda b, h, pg: (pg[(b * n_h + h) * n_tiles + i], h, 0, 0))

    small = pl.BlockSpec((None, None, n_q, HEAD_DIM), lambda b, h, pg: (b, h, 0, 0))
    grid_spec = pltpu.PrefetchScalarGridSpec(
        num_scalar_prefetch=1,
        grid=(db, n_h),
        in_specs=[small, small, small] + [tile_spec(i) for i in range(n_tiles)] * 2,
        out_specs=small,
    )
    return pl.pallas_call(
        functools.partial(_sample_attn_kernel, n_q=n_q, tiles_per_q=tiles_per_q),
        grid_spec=grid_spec,
        out_shape=jax.ShapeDtypeStruct((db, n_h, n_q, HEAD_DIM), F32),
        compiler_params=_cparams(2),
        name="sample_attn",
    )(pages, q, k_new, v_new, *([cache_kt] * n_tiles), *([cache_vt] * n_tiles))


def _even_layer(x, conv_prev, gla_prev, p, tm, conv_tt, gla_c, ffn_tm, ffn_tf):
    b, t, d = x.shape
    xf = x.reshape(b * t, d)
    glu, q, k, v, gate, la = _inproj(xf, p["norm_mix"], p["w_in"], p["w_a2"], p["b_a"], tm)
    as_bt = lambda a: a.reshape(b, t, a.shape[-1])
    conv_out, conv_state = _conv_module(as_bt(glu), conv_prev, p["conv_w"], p["conv_b"], p["ln_g"], p["ln_b"], conv_tt)
    o, gla_state = _gla(as_bt(q), as_bt(k), as_bt(la), as_bt(v), as_bt(gate),
                        gla_prev.reshape(b, GLA_HEADS * GLA_DK, GLA_DV), p["gla_norm_g"], gla_c, 4)
    h = _outproj(xf, [conv_out.reshape(b * t, -1), o.reshape(b * t, -1)], p["w_out"], tm)
    h = _ffn(h, p["norm_ffn"], p["ffn_w_gate"], p["ffn_w_up"], p["ffn_w_down"], ffn_tm, ffn_tf)
    return h.reshape(b, t, d), conv_state, gla_state.reshape(b, GLA_HEADS, GLA_DK, GLA_DV)


def kernel(x_prompt, x_sample, state_conv, state_gla, cache_k, cache_v, page_table, norm_mix_even, w_in_even, conv_w, conv_b, conv_ln_g, conv_ln_b, gla_w_a2, gla_b_a, gla_norm_g, w_out_even, norm_ffn_even, ffn_w_gate, ffn_w_up, ffn_w_down, norm_mix_odd, w_qkv_odd, w_out_odd, norm_ffn_odd, router_w, moe_w_gate, moe_w_up, moe_w_down, final_norm_g):
    bp, tp, d = x_prompt.shape
    db, ts, _ = x_sample.shape
    even = dict(norm_mix=norm_mix_even[0], w_in=w_in_even[0], conv_w=conv_w[0], conv_b=conv_b[0],
                ln_g=conv_ln_g[0], ln_b=conv_ln_b[0], w_a2=gla_w_a2[0], b_a=gla_b_a[0],
                gla_norm_g=gla_norm_g[0], w_out=w_out_even[0], norm_ffn=norm_ffn_even[0],
                ffn_w_gate=ffn_w_gate, ffn_w_up=ffn_w_up, ffn_w_down=ffn_w_down)

    hp, conv_p, gla_p = _even_layer(
        x_prompt, jnp.zeros((bp, CONV_W - 1, D_CONV), F32), jnp.zeros((bp, GLA_HEADS, GLA_DK, GLA_DV), F32),
        even, tm=512, conv_tt=256, gla_c=GLA_CHUNK, ffn_tm=1024, ffn_tf=256)

    m_s = db * ts
    xs = x_sample.reshape(m_s, d)
    glu, q, k, v, gate, la = _inproj(xs, even["norm_mix"], even["w_in"], even["w_a2"], even["b_a"], m_s)
    as_bt = lambda a: a.reshape(db, ts, a.shape[-1])
    conv_out_s, conv_s = _conv_module(as_bt(glu), state_conv[0], even["conv_w"], even["conv_b"],
                                      even["ln_g"], even["ln_b"], ts)
    pad_t = 16
    pad8 = lambda a: jnp.pad(as_bt(a), ((0, 0), (0, pad_t - ts), (0, 0)))
    o_s, gla_s = _gla(pad8(q), pad8(k), pad8(la), pad8(v), pad8(gate),
                      state_gla[0].reshape(db, GLA_HEADS * GLA_DK, GLA_DV), even["gla_norm_g"], pad_t, 4)
    hs = _outproj(xs, [conv_out_s.reshape(m_s, -1), o_s[:, :ts].reshape(m_s, -1)], even["w_out"], m_s)
    hs = _ffn(hs, even["norm_ffn"], ffn_w_gate, ffn_w_up, ffn_w_down, m_s, 256)
    gla_s = gla_s.reshape(db, GLA_HEADS, GLA_DK, GLA_DV)

    qp, k_rows, v_rows = _qkv_rope(hp, norm_mix_odd[0], w_qkv_odd[0], jnp.arange(tp), 256, True)
    op = _moba_prompt(qp, k_rows, v_rows)
    moe_wg, moe_wu, moe_wd = moe_w_gate[0].astype(BF16), moe_w_up[0].astype(BF16), moe_w_down[0].astype(BF16)
    hp2, xn_p, gate_p, dest_p = _route(hp.reshape(bp * tp, d), op.reshape(bp * tp, d), w_out_odd[0],
                                       norm_ffn_odd[0], router_w[0], 1024)
    y_prompt = _experts(hp2, xn_p, gate_p, dest_p, moe_wg, moe_wu, moe_wd, final_norm_g,
                        tm=1024, tf=1792, rb=128).reshape(bp, tp, d)

    n_blocks = page_table.shape[1] * PAGE_SIZE // MOBA_BLOCK
    past_len = page_table.shape[1] * PAGE_SIZE
    pos_s = past_len + (jnp.arange(m_s) % ts)
    qs, ks, vs = _qkv_rope(hs.reshape(1, m_s, d), norm_mix_odd[0], w_qkv_odd[0], pos_s, m_s, False)
    to_heads = lambda a: a.reshape(db, ts, ATT_HEADS, HEAD_DIM).transpose(0, 2, 1, 3)
    qs, ks, vs = to_heads(qs), to_heads(ks), to_heads(vs)
    cache_kt, cache_vt = jnp.swapaxes(cache_k[0], -1, -2), jnp.swapaxes(cache_v[0], -1, -2)
    tiles_per_q = MOBA_TOPK * (MOBA_BLOCK // PAGE_SIZE)
    pages = _sample_select(qs, cache_kt, page_table, n_blocks)[:, :, :tiles_per_q].reshape(-1)
    os_ = _sample_attn(qs, ks, vs, cache_kt, cache_vt, pages)
    os_ = os_.transpose(0, 2, 1, 3).reshape(m_s, d)
    hs2 = _outproj(hs, [os_], w_out_odd[0], m_s)
    y_sample = _ffn(hs2, norm_ffn_odd[0], moe_wg, moe_wu, moe_wd, m_s, 512,
                    router_w=router_w[0], final_g=final_norm_g).reshape(db, ts, d)

    return (y_prompt, y_sample, conv_p[None], gla_p[None], k_rows[None], v_rows[None],
            conv_s[None], gla_s[None], ks[None], vs[None])
```

```python
import functools

import jax
import jax.numpy as jnp
import numpy as np
from jax import lax
from jax.experimental import pallas as pl
from jax.experimental.pallas import tpu as pltpu

F32 = jnp.float32
BF16 = jnp.bfloat16
HIGHEST = lax.Precision.HIGHEST

LANES = 128
SUBLANES = 8
NORM_EPS = 1e-6
D_MODEL = 1024
D_CONV = 512
CONV_W = 31
CONV_HALO = 32
GLA_HEADS = 4
GLA_DK = 64
GLA_DV = 128
GLA_TAU = 16.0
GLA_CHUNK = 64
ATT_HEADS = 16
HEAD_DIM = 64
MOBA_BLOCK = 256
MOBA_TOPK = 3
PAGE_SIZE = 128
ROPE_THETA = 10000.0
N_EXPERTS = 8
VMEM_LIMIT = 60 * 1024 * 1024


def _cparams(n_axes):
    return pltpu.CompilerParams(dimension_semantics=("arbitrary",) * n_axes,
                                vmem_limit_bytes=VMEM_LIMIT)


def _rms(x, g):
    return x * lax.rsqrt(jnp.mean(x * x, axis=-1, keepdims=True) + NORM_EPS) * g


def _silu(x):
    return x * jax.nn.sigmoid(x)


def _bdot(a, b):
    return jnp.dot(a.astype(BF16), b.astype(BF16), preferred_element_type=F32)


def _bdot_t(a, b):
    return lax.dot_general(a.astype(BF16), b.astype(BF16), (((1,), (1,)), ((), ())),
                           preferred_element_type=F32)


def _dot_split(a, b):
    a_hi, b_hi = a.astype(BF16), b.astype(BF16)
    a_lo = (a - a_hi.astype(F32)).astype(BF16)
    b_lo = (b - b_hi.astype(F32)).astype(BF16)
    dot = functools.partial(jnp.dot, preferred_element_type=F32)
    return dot(a_hi, b_hi) + (dot(a_hi, b_lo) + dot(a_lo, b_hi))


def _tdot(a, b):
    return lax.dot_general(a.astype(BF16), b.astype(BF16), (((0,), (0,)), ((), ())),
                           preferred_element_type=F32)


def _inproj_kernel(x_ref, g_ref, w_ref, wa1_ref, wa2_ref, ba_ref,
                   glu_ref, q_ref, k_ref, v_ref, gate_ref, la_ref, wbf_ref, wa1bf_ref):
    @pl.when(pl.program_id(0) == 0)
    def _():
        wbf_ref[...] = w_ref[...].astype(BF16)
        wa1bf_ref[...] = wa1_ref[...].astype(BF16)

    xn = _rms(x_ref[...], g_ref[...]).astype(BF16)
    y = jnp.dot(xn, wbf_ref[...], preferred_element_type=F32)
    glu_ref[...] = y[:, :D_CONV] * jax.nn.sigmoid(y[:, D_CONV:2 * D_CONV])
    q_ref[...] = y[:, 1024:1280] * (GLA_DK ** -0.5)
    k_ref[...] = y[:, 1280:1536]
    v_ref[...] = y[:, 1536:2048]
    gate_ref[...] = y[:, 2048:2560]
    a = jnp.dot(xn, wa1bf_ref[...], preferred_element_type=F32)
    z = jnp.dot(a, wa2_ref[...], precision=HIGHEST, preferred_element_type=F32) + ba_ref[...]
    log_sig = jnp.minimum(z, 0.0) - jnp.log(1.0 + jnp.exp(-jnp.abs(z)))
    la_ref[...] = log_sig * (1.0 / GLA_TAU)


def _inproj(x, g, w_in, w_a2, b_a, tm):
    m = x.shape[0]
    n_main = 2560
    wa1 = jnp.pad(w_in[:, n_main:], ((0, 0), (0, LANES - (w_in.shape[1] - n_main))))
    wa2 = jnp.pad(w_a2, ((0, LANES - w_a2.shape[0]), (0, 0)))
    row = lambda i: (i, 0)
    fixed = lambda i: (0, 0)
    outs = [(D_CONV, F32), (256, F32), (256, F32), (512, F32), (512, F32), (256, F32)]
    return pl.pallas_call(
        _inproj_kernel,
        grid=(m // tm,),
        in_specs=[pl.BlockSpec((tm, D_MODEL), row),
                  pl.BlockSpec((1, D_MODEL), fixed),
                  pl.BlockSpec((D_MODEL, n_main), fixed),
                  pl.BlockSpec((D_MODEL, LANES), fixed),
                  pl.BlockSpec((LANES, 256), fixed),
                  pl.BlockSpec((1, 256), fixed)],
        out_specs=[pl.BlockSpec((tm, n), row) for n, _ in outs],
        out_shape=[jax.ShapeDtypeStruct((m, n), dt) for n, dt in outs],
        scratch_shapes=[pltpu.VMEM((D_MODEL, n_main), BF16), pltpu.VMEM((D_MODEL, LANES), BF16)],
        compiler_params=_cparams(1),
        name="inproj",
    )(x, g.reshape(1, -1), w_in, wa1, wa2, b_a.reshape(1, -1))


def _conv_kernel(glu_ref, prev_ref, w_ref, b_ref, lg_ref, lb_ref, out_ref, st_ref, full_ref, ph_ref, *, tt, rc):
    t = pl.program_id(1)
    lo = CONV_HALO - (CONV_W - 1)

    @pl.when(t == 0)
    def _():
        full_ref[0:lo, :] = jnp.zeros((lo, D_CONV), F32)
        full_ref[lo:CONV_HALO, :] = prev_ref[...]

    full_ref[CONV_HALO:CONV_HALO + tt, :] = glu_ref[...]
    span = ph_ref.shape[1]
    for r in range(1, SUBLANES):
        ph_ref[r - 1] = full_ref[r:r + span, :]

    def slab(c, j):
        a, r = divmod(lo + j, SUBLANES)
        start = c * rc + a * SUBLANES
        return full_ref[start:start + rc, :] if r == 0 else ph_ref[r - 1, start:start + rc, :]

    for c in range(tt // rc):
        acc = slab(c, 0) * w_ref[0:1, :]
        for j in range(1, CONV_W):
            acc = acc + slab(c, j) * w_ref[j:j + 1, :]
        y = acc + b_ref[...]
        mu = jnp.mean(y, axis=-1, keepdims=True)
        d = y - mu
        var = jnp.mean(d * d, axis=-1, keepdims=True)
        yn = d * lax.rsqrt(var + NORM_EPS) * lg_ref[...] + lb_ref[...]
        out_ref[c * rc:(c + 1) * rc, :] = _silu(yn).astype(out_ref.dtype)

    tail = full_ref[tt + lo:tt + CONV_HALO, :]

    @pl.when(t == pl.num_programs(1) - 1)
    def _():
        st_ref[...] = tail

    full_ref[lo:CONV_HALO, :] = tail


def _conv_module(glu, prev, conv_w, conv_b, ln_g, ln_b, tt):
    b, t, _ = glu.shape
    rc = min(tt, 32)
    fixed = lambda i, j: (0, 0)
    return pl.pallas_call(
        functools.partial(_conv_kernel, tt=tt, rc=rc),
        grid=(b, t // tt),
        in_specs=[pl.BlockSpec((None, tt, D_CONV), lambda i, j: (i, j, 0)),
                  pl.BlockSpec((None, CONV_W - 1, D_CONV), lambda i, j: (i, 0, 0)),
                  pl.BlockSpec((CONV_W, D_CONV), fixed),
                  pl.BlockSpec((1, D_CONV), fixed),
                  pl.BlockSpec((1, D_CONV), fixed),
                  pl.BlockSpec((1, D_CONV), fixed)],
        out_specs=[pl.BlockSpec((None, tt, D_CONV), lambda i, j: (i, j, 0)),
                   pl.BlockSpec((None, CONV_W - 1, D_CONV), lambda i, j: (i, 0, 0))],
        out_shape=[jax.ShapeDtypeStruct((b, t, D_CONV), BF16),
                   jax.ShapeDtypeStruct((b, CONV_W - 1, D_CONV), F32)],
        scratch_shapes=[pltpu.VMEM((CONV_HALO + tt, D_CONV), F32),
                        pltpu.VMEM((SUBLANES - 1, CONV_HALO + tt - SUBLANES, D_CONV), F32)],
        compiler_params=_cparams(2),
        name="conv_module",
    )(glu, prev, conv_w, conv_b.reshape(1, -1), ln_g.reshape(1, -1), ln_b.reshape(1, -1))


def _gla_levels(c):
    levels, g_big = [], c
    while g_big > 1:
        g_small = max(g_big // 4, 1)
        levels.append((g_big, g_small))
        g_big = g_small
    return levels


def _gla_decay_matrices(c):
    t = np.arange(c)[:, None]
    s = np.arange(c)[None, :]
    mats = [(s <= t), (s > t)]
    for g_big, g_small in _gla_levels(c):
        sub = (t % g_big) // g_small
        ref_q = (t // g_big) * g_big + sub * g_small - 1
        mats.append((sub >= 1) & (s > ref_q) & (s <= t))
        for i in range(1, g_big // g_small):
            ref_k = (t // g_big) * g_big + i * g_small - 1
            mats.append(((t % g_big) < i * g_small) & (s > t) & (s <= ref_k))
    return np.concatenate([m.astype(np.float32) for m in mats], axis=0)


def _gla_kernel(q_ref, k_ref, la_ref, v_ref, gate_ref, s0_ref, ng_ref, dmat_ref,
                out_ref, st_ref, s_ref, *, c):
    ci = pl.program_id(1)

    @pl.when(ci == 0)
    def _():
        s_ref[...] = s0_ref[...]

    for bi in range(q_ref.shape[0]):
        out, new_state = _gla_chunk(q_ref[bi], k_ref[bi], la_ref[bi], v_ref[bi], gate_ref[bi], s_ref[bi],
                                    ng_ref[...], dmat_ref[...], c)
        out_ref[bi] = out.astype(out_ref.dtype)
        s_ref[bi] = new_state

    @pl.when(ci == pl.num_programs(1) - 1)
    def _():
        st_ref[...] = s_ref[...]


def _gla_chunk(q, k, la, v, gate, state, norm_g, dmat, c):
    hk = GLA_HEADS * GLA_DK
    la1 = la.astype(BF16)
    la2 = (la - la1.astype(F32)).astype(BF16)
    e_all = (jnp.dot(dmat, la1, preferred_element_type=F32)
             + jnp.dot(dmat, la2, preferred_element_type=F32))

    def e_blk(i):
        return e_all[i * c:(i + 1) * c, :]

    lane_head = lax.broadcasted_iota(jnp.int32, (c, hk), 1) // GLA_DK
    head_masks = [(lane_head == h).astype(F32) for h in range(GLA_HEADS)]

    def stack_heads(x):
        return jnp.concatenate([x * m for m in head_masks], axis=0).astype(BF16)

    row_t = lax.broadcasted_iota(jnp.int32, (GLA_HEADS * c, c), 0) % c
    col_s = lax.broadcasted_iota(jnp.int32, (GLA_HEADS * c, c), 1)
    row_k = lax.broadcasted_iota(jnp.int32, (c, hk), 0)

    o_st = jnp.dot(stack_heads(q * jnp.exp(e_blk(0))), state.astype(BF16),
                   preferred_element_type=F32)

    att = jnp.where(row_t == col_s, _bdot_t(stack_heads(q), k), 0.0)
    idx = 2
    for g_big, g_small in _gla_levels(c):
        q_lvl = stack_heads(q * jnp.exp(e_blk(idx)))
        idx += 1
        n_var = g_big // g_small - 1
        k_vars = []
        for i in range(1, n_var + 1):
            k_vars.append(jnp.where((row_k % g_big) < i * g_small, k * jnp.exp(e_blk(idx)), 0.0))
            idx += 1
        p = _bdot_t(q_lvl, jnp.concatenate(k_vars, axis=0))
        wide_t = lax.broadcasted_iota(jnp.int32, p.shape, 0) % c
        wide_col = lax.broadcasted_iota(jnp.int32, p.shape, 1)
        keep = ((wide_t // g_big) == ((wide_col % c) // g_big)) & ((wide_t % g_big) // g_small == wide_col // c + 1)
        p = jnp.where(keep, p, 0.0)
        for i in range(n_var):
            att = att + p[:, i * c:(i + 1) * c]

    att = att.astype(BF16)
    outs = []
    for h in range(GLA_HEADS):
        v_h = v[:, h * GLA_DV:(h + 1) * GLA_DV]
        o_h = o_st[h * c:(h + 1) * c, :] + jnp.dot(att[h * c:(h + 1) * c, :], v_h.astype(BF16),
                                                   preferred_element_type=F32)
        o_h = _rms(o_h, norm_g[:, h * GLA_DV:(h + 1) * GLA_DV])
        outs.append(o_h * _silu(gate[:, h * GLA_DV:(h + 1) * GLA_DV]))

    kd = (k * jnp.exp(e_blk(1))).astype(BF16)
    ones = jnp.ones((c, GLA_DV), BF16)
    b_last = _tdot(la1, ones) + _tdot(la2, ones)
    row_head = lax.broadcasted_iota(jnp.int32, (hk, GLA_DV), 0) // GLA_DK
    upd = jnp.zeros((hk, GLA_DV), F32)
    for h in range(GLA_HEADS):
        kv = _tdot(kd, v[:, h * GLA_DV:(h + 1) * GLA_DV])
        upd = upd + jnp.where(row_head == h, kv, 0.0)
    return jnp.concatenate(outs, axis=1), jnp.exp(b_last) * state + upd


def _gla(q, k, la, v, gate, s0, norm_g, c, nbs):
    b, t, hk = q.shape
    assert b % nbs == 0 and t % c == 0
    dv_all = v.shape[-1]
    dmat = jnp.asarray(_gla_decay_matrices(c), dtype=BF16)
    tok = lambda n: pl.BlockSpec((nbs, c, n), lambda i, j: (i, j, 0))
    fixed = lambda i, j: (0, 0)
    return pl.pallas_call(
        functools.partial(_gla_kernel, c=c),
        grid=(b // nbs, t // c),
        in_specs=[tok(hk), tok(hk), tok(hk), tok(dv_all), tok(dv_all),
                  pl.BlockSpec((nbs, hk, GLA_DV), lambda i, j: (i, 0, 0)),
                  pl.BlockSpec((1, dv_all), fixed),
                  pl.BlockSpec(dmat.shape, fixed)],
        out_specs=[tok(dv_all), pl.BlockSpec((nbs, hk, GLA_DV), lambda i, j: (i, 0, 0))],
        out_shape=[jax.ShapeDtypeStruct((b, t, dv_all), BF16),
                   jax.ShapeDtypeStruct((b, hk, GLA_DV), F32)],
        scratch_shapes=[pltpu.VMEM((nbs, hk, GLA_DV), F32)],
        compiler_params=_cparams(2),
        name="gla",
    )(q, k, la, v, gate, s0, norm_g.reshape(1, -1), dmat)


def _outproj_kernel(*refs, n_parts):
    x_ref = refs[0]
    a_refs = refs[1:1 + n_parts]
    w_ref = refs[1 + n_parts]
    out_ref = refs[2 + n_parts]
    wbf_ref = refs[3 + n_parts]

    @pl.when(pl.program_id(0) == 0)
    def _():
        wbf_ref[...] = w_ref[...].astype(BF16)

    acc = x_ref[...]
    off = 0
    for a_ref in a_refs:
        n = a_ref.shape[-1]
        acc = acc + jnp.dot(a_ref[...].astype(BF16), wbf_ref[off:off + n, :], preferred_element_type=F32)
        off += n
    out_ref[...] = acc


def _outproj(x, parts, w, tm):
    m = x.shape[0]
    row = lambda i: (i, 0)
    return pl.pallas_call(
        functools.partial(_outproj_kernel, n_parts=len(parts)),
        grid=(m // tm,),
        in_specs=[pl.BlockSpec((tm, D_MODEL), row)]
                 + [pl.BlockSpec((tm, p.shape[-1]), row) for p in parts]
                 + [pl.BlockSpec(w.shape, lambda i: (0, 0))],
        out_specs=pl.BlockSpec((tm, D_MODEL), row),
        out_shape=jax.ShapeDtypeStruct((m, D_MODEL), F32),
        scratch_shapes=[pltpu.VMEM(w.shape, BF16)],
        compiler_params=_cparams(1),
        name="outproj",
    )(x, *parts, w)


def _ffn_kernel(*refs, routed, final_norm, emit_bf16):
    it = iter(refs)
    h_ref, g_ref = next(it), next(it)
    rw_ref = next(it) if routed else None
    wg_ref, wu_ref, wd_ref = next(it), next(it), next(it)
    fg_ref = next(it) if final_norm else None
    out_ref = next(it)
    bf_refs = [next(it), next(it), next(it)] if emit_bf16 else None
    xn_ref, acc_ref = next(it), next(it)
    gate_ref = next(it) if routed else None
    e, f = pl.program_id(1), pl.program_id(2)

    @pl.when((e == 0) & (f == 0))
    def _():
        xn = _rms(h_ref[...], g_ref[...])
        xn_ref[...] = xn.astype(BF16)
        acc_ref[...] = jnp.zeros_like(acc_ref)
        if routed:
            gate_ref[...] = _top2_gates(jnp.dot(xn, rw_ref[...], precision=HIGHEST, preferred_element_type=F32))

    xn = xn_ref[...]
    wg, wu, wd = wg_ref[...].astype(BF16), wu_ref[...].astype(BF16), wd_ref[...].astype(BF16)
    if emit_bf16:
        for ref, w in zip(bf_refs, (wg, wu, wd)):
            ref[...] = w
    hid = _silu(jnp.dot(xn, wg, preferred_element_type=F32)) * jnp.dot(xn, wu, preferred_element_type=F32)
    if routed:
        lane = lax.broadcasted_iota(jnp.int32, gate_ref.shape, 1)
        hid = hid * jnp.sum(jnp.where(lane == e, gate_ref[...], 0.0), axis=-1, keepdims=True)
    acc_ref[...] += jnp.dot(hid.astype(BF16), wd, preferred_element_type=F32)

    @pl.when((e == pl.num_programs(1) - 1) & (f == pl.num_programs(2) - 1))
    def _():
        y = h_ref[...] + acc_ref[...]
        if final_norm:
            y = _rms(y, fg_ref[...])
        out_ref[...] = y


def _ffn(h, g, w_gate, w_up, w_down, tm, tf, router_w=None, final_g=None, emit_bf16=False):
    m = h.shape[0]
    n_e, _, d_ff = w_gate.shape
    assert not emit_bf16 or m == tm
    routed = router_w is not None
    final_norm = final_g is not None
    row = lambda i, e, f: (i, 0)
    fixed = lambda i, e, f: (0, 0)
    args = [h, g.reshape(1, -1)]
    in_specs = [pl.BlockSpec((tm, D_MODEL), row), pl.BlockSpec((1, D_MODEL), fixed)]
    if routed:
        args.append(jnp.pad(router_w, ((0, 0), (0, LANES - router_w.shape[1]))))
        in_specs.append(pl.BlockSpec((D_MODEL, LANES), fixed))
    args += [w_gate, w_up, w_down]
    w_specs = [pl.BlockSpec((None, D_MODEL, tf), lambda i, e, f: (e, 0, f)),
               pl.BlockSpec((None, D_MODEL, tf), lambda i, e, f: (e, 0, f)),
               pl.BlockSpec((None, tf, D_MODEL), lambda i, e, f: (e, f, 0))]
    in_specs += w_specs
    if final_norm:
        args.append(final_g.reshape(1, -1))
        in_specs.append(pl.BlockSpec((1, D_MODEL), fixed))
    scratch = [pltpu.VMEM((tm, D_MODEL), BF16), pltpu.VMEM((tm, D_MODEL), F32)]
    if routed:
        scratch.append(pltpu.VMEM((tm, LANES), F32))
    out_specs = [pl.BlockSpec((tm, D_MODEL), row)]
    out_shape = [jax.ShapeDtypeStruct((m, D_MODEL), F32)]
    if emit_bf16:
        out_specs += w_specs
        out_shape += [jax.ShapeDtypeStruct(w.shape, BF16) for w in (w_gate, w_up, w_down)]
    outs = pl.pallas_call(
        functools.partial(_ffn_kernel, routed=routed, final_norm=final_norm, emit_bf16=emit_bf16),
        grid=(m // tm, n_e, d_ff // tf),
        in_specs=in_specs,
        out_specs=out_specs,
        out_shape=out_shape,
        scratch_shapes=scratch,
        compiler_params=_cparams(3),
        name="moe_ffn" if routed else "ffn",
    )(*args)
    return outs if emit_bf16 else outs[0]


def _top2_gates(logits):
    lane = lax.broadcasted_iota(jnp.int32, logits.shape, 1)
    neg = jnp.float32(-jnp.inf)
    logits = jnp.where(lane < N_EXPERTS, logits, neg)
    m1 = jnp.max(logits, axis=-1, keepdims=True)
    i1 = jnp.min(jnp.where(logits == m1, lane, LANES), axis=-1, keepdims=True)
    rest = jnp.where(lane == i1, neg, logits)
    m2 = jnp.max(rest, axis=-1, keepdims=True)
    i2 = jnp.min(jnp.where(rest == m2, lane, LANES), axis=-1, keepdims=True)
    e2 = jnp.exp(m2 - m1)
    g1 = 1.0 / (1.0 + e2)
    g2 = e2 / (1.0 + e2)
    return jnp.where(lane == i1, g1, 0.0) + jnp.where(lane == i2, g2, 0.0)


def _route_kernel(x_ref, a_ref, w_ref, g_ref, rw_ref, h_ref, xn_ref, gate_ref, dest_ref, wbf_ref, ltri_ref):
    tm = x_ref.shape[0]

    @pl.when(pl.program_id(0) == 0)
    def _():
        wbf_ref[...] = w_ref[...].astype(BF16)
        earlier = (lax.broadcasted_iota(jnp.int32, (tm, tm), 1) < lax.broadcasted_iota(jnp.int32, (tm, tm), 0))
        ltri_ref[...] = jnp.where(earlier, 1.0, 0.0).astype(BF16)

    h = x_ref[...] + jnp.dot(a_ref[...].astype(BF16), wbf_ref[...], preferred_element_type=F32)
    h_ref[...] = h
    xn = _rms(h, g_ref[...])
    xn_ref[...] = xn.astype(BF16)
    gate = _top2_gates(_dot_split(xn, rw_ref[...]))
    gate_ref[...] = gate
    routed = gate > 0.0
    rank = jnp.dot(ltri_ref[...], jnp.where(routed, 1.0, 0.0).astype(BF16), preferred_element_type=F32)
    dest_ref[...] = jnp.where(routed, rank, -1.0)


def _route(x, a, w_out, g, router_w, tm):
    m = x.shape[0]
    row = lambda i: (i, 0)
    fixed = lambda i: (0, 0)
    rw = jnp.pad(router_w, ((0, 0), (0, LANES - router_w.shape[1])))
    return pl.pallas_call(
        _route_kernel,
        grid=(m // tm,),
        in_specs=[pl.BlockSpec((tm, D_MODEL), row), pl.BlockSpec((tm, D_MODEL), row),
                  pl.BlockSpec(w_out.shape, fixed), pl.BlockSpec((1, D_MODEL), fixed),
                  pl.BlockSpec((D_MODEL, LANES), fixed)],
        out_specs=[pl.BlockSpec((tm, D_MODEL), row), pl.BlockSpec((tm, D_MODEL), row),
                   pl.BlockSpec((tm, LANES), row), pl.BlockSpec((tm, LANES), row)],
        out_shape=[jax.ShapeDtypeStruct((m, D_MODEL), F32), jax.ShapeDtypeStruct((m, D_MODEL), BF16),
                   jax.ShapeDtypeStruct((m, LANES), F32), jax.ShapeDtypeStruct((m, LANES), F32)],
        scratch_shapes=[pltpu.VMEM(w_out.shape, BF16), pltpu.VMEM((tm, tm), BF16)],
        compiler_params=_cparams(1),
        name="route",
    )(x, a, w_out, g.reshape(1, -1), rw)


def _experts_kernel(cnt_ref, xn_ref, gate_ref, dest_ref, h_ref, wg_ref, wu_ref, wd_ref, fg_ref, out_ref,
                    dest_t_ref, xc_ref, yc_ref, *, rb, mb):
    i, e, f = pl.program_id(0), pl.program_id(1), pl.program_id(2)
    n_e, n_f = pl.num_programs(1), pl.num_programs(2)
    tm = xn_ref.shape[0]

    @pl.when((e == 0) & (f == 0))
    def _():
        dest_t_ref[...] = jnp.transpose(dest_ref[...])
        out_ref[...] = h_ref[...]

    count = cnt_ref[i * n_e + e]
    n_blk = (count + rb - 1) // rb
    slot_row = dest_t_ref[pl.ds(e, 1), :]
    lane = lax.broadcasted_iota(jnp.int32, (tm, LANES), 1)
    slot_col = jnp.sum(jnp.where(lane == e, dest_ref[...], 0.0), axis=-1, keepdims=True)

    def rows(blk):
        return pl.ds(pl.multiple_of(blk * rb, rb), rb)

    n_mv = (count + mb - 1) // mb

    def mrows(blk):
        return pl.ds(pl.multiple_of(blk * mb, mb), mb)

    @pl.when(f == 0)
    def _():
        def gather(blk, carry):
            want = (blk * mb + lax.broadcasted_iota(jnp.int32, (mb, tm), 0)).astype(F32)
            pick = jnp.where(slot_row == want, 1.0, 0.0).astype(BF16)
            xc_ref[mrows(blk), :] = jnp.dot(pick, xn_ref[...], preferred_element_type=F32).astype(BF16)
            yc_ref[mrows(blk), :] = jnp.zeros((mb, D_MODEL), F32)
            return carry
        lax.fori_loop(0, n_mv, gather, 0)

    def expert(blk, carry):
        xc = xc_ref[rows(blk), :]
        hid = _silu(jnp.dot(xc, wg_ref[...], preferred_element_type=F32)) \
            * jnp.dot(xc, wu_ref[...], preferred_element_type=F32)
        yc_ref[rows(blk), :] += jnp.dot(hid.astype(BF16), wd_ref[...], preferred_element_type=F32)
        return carry
    lax.fori_loop(0, n_blk, expert, 0)

    @pl.when(f == n_f - 1)
    def _():
        gate_col = jnp.sum(jnp.where(lane == e, gate_ref[...], 0.0), axis=-1, keepdims=True)

        def scatter(blk, carry):
            want = (blk * mb + lax.broadcasted_iota(jnp.int32, (tm, mb), 1)).astype(F32)
            place = jnp.where(slot_col == want, 1.0, 0.0).astype(BF16)
            out_ref[...] += gate_col * jnp.dot(place, yc_ref[mrows(blk), :].astype(BF16),
                                               preferred_element_type=F32)
            return carry
        lax.fori_loop(0, n_mv, scatter, 0)

    @pl.when((e == n_e - 1) & (f == n_f - 1))
    def _():
        out_ref[...] = _rms(out_ref[...], fg_ref[...])


def _experts(h, xn, gate, dest, w_gate, w_up, w_down, final_g, tm, tf, rb, mb):
    m = h.shape[0]
    n_e, _, d_ff = w_gate.shape
    assert tm % mb == 0 and mb % rb == 0 and m % tm == 0 and d_ff % tf == 0
    counts = jnp.sum((dest[:, :n_e] >= 0.0).reshape(m // tm, tm, n_e), axis=1, dtype=jnp.int32).reshape(-1)
    row = lambda i, e, f, c: (i, 0)
    grid_spec = pltpu.PrefetchScalarGridSpec(
        num_scalar_prefetch=1,
        grid=(m // tm, n_e, d_ff // tf),
        in_specs=[pl.BlockSpec((tm, D_MODEL), row), pl.BlockSpec((tm, LANES), row), pl.BlockSpec((tm, LANES), row),
                  pl.BlockSpec((tm, D_MODEL), row),
                  pl.BlockSpec((None, D_MODEL, tf), lambda i, e, f, c: (e, 0, f)),
                  pl.BlockSpec((None, D_MODEL, tf), lambda i, e, f, c: (e, 0, f)),
                  pl.BlockSpec((None, tf, D_MODEL), lambda i, e, f, c: (e, f, 0)),
                  pl.BlockSpec((1, D_MODEL), lambda i, e, f, c: (0, 0))],
        out_specs=pl.BlockSpec((tm, D_MODEL), row),
        scratch_shapes=[pltpu.VMEM((LANES, tm), F32), pltpu.VMEM((tm, D_MODEL), BF16),
                        pltpu.VMEM((tm, D_MODEL), F32)],
    )
    return pl.pallas_call(
        functools.partial(_experts_kernel, rb=rb, mb=mb),
        grid_spec=grid_spec,
        out_shape=jax.ShapeDtypeStruct((m, D_MODEL), F32),
        compiler_params=_cparams(3),
        name="experts",
    )(counts, xn, gate, dest, h, w_gate, w_up, w_down, final_g.reshape(1, -1))


def _rope_tables(pos):
    half = HEAD_DIM // 2
    inv = jnp.power(ROPE_THETA, -jnp.arange(half, dtype=F32) / half)
    ang = pos.astype(F32)[:, None] * inv[None, :]
    cos, sin = jnp.cos(ang), jnp.sin(ang)
    cos_t = jnp.concatenate([cos, cos, cos, cos], axis=1)
    sin_t = jnp.concatenate([-sin, sin, -sin, sin], axis=1)
    return cos_t, sin_t


def _rope_apply(y, cos_t, sin_t):
    half = HEAD_DIM // 2
    first = (lax.broadcasted_iota(jnp.int32, (y.shape[0], LANES), 1) % HEAD_DIM) < half
    cols = []
    for c in range(y.shape[1] // LANES):
        yc = y[:, c * LANES:(c + 1) * LANES]
        swapped = jnp.where(first, pltpu.roll(yc, LANES - half, 1), pltpu.roll(yc, half, 1))
        cols.append(yc * cos_t + swapped * sin_t)
    return jnp.concatenate(cols, axis=1)


def _qkv_kernel(x_ref, g_ref, w_ref, cos_ref, sin_ref, q_ref, k_ref, v_ref, wbf_ref, *, head_major):
    @pl.when((pl.program_id(0) == 0) & (pl.program_id(1) == 0))
    def _():
        wbf_ref[...] = w_ref[...].astype(BF16)

    xn = _rms(x_ref[...], g_ref[...]).astype(BF16)
    y = jnp.dot(xn, wbf_ref[...], preferred_element_type=F32)
    cos_t, sin_t = cos_ref[...], sin_ref[...]
    q = _rope_apply(y[:, :D_MODEL], cos_t, sin_t) * (HEAD_DIM ** -0.5)
    k = _rope_apply(y[:, D_MODEL:2 * D_MODEL], cos_t, sin_t)
    v = y[:, 2 * D_MODEL:]
    if head_major:
        tm = q.shape[0]
        for h in range(ATT_HEADS):
            sl = slice(h * HEAD_DIM, (h + 1) * HEAD_DIM)
            q_ref[h] = q[:, sl]
            for p in range(tm // PAGE_SIZE):
                rows = slice(p * PAGE_SIZE, (p + 1) * PAGE_SIZE)
                k_ref[p, h] = k[rows, sl]
                v_ref[p, h] = v[rows, sl]
    else:
        q_ref[...] = q
        k_ref[...] = k
        v_ref[...] = v


def _qkv_rope(x, g, w_qkv, pos, tm, head_major):
    b, t, _ = x.shape
    cos_t, sin_t = _rope_tables(pos)
    fixed = lambda i, j: (0, 0)
    if head_major:
        npg = tm // PAGE_SIZE
        out_specs = [pl.BlockSpec((None, ATT_HEADS, tm, HEAD_DIM), lambda i, j: (i, 0, j, 0)),
                     pl.BlockSpec((None, npg, ATT_HEADS, PAGE_SIZE, HEAD_DIM), lambda i, j: (i, j, 0, 0, 0)),
                     pl.BlockSpec((None, npg, ATT_HEADS, PAGE_SIZE, HEAD_DIM), lambda i, j: (i, j, 0, 0, 0))]
        kv_shape = (b, t // PAGE_SIZE, ATT_HEADS, PAGE_SIZE, HEAD_DIM)
        out_shape = [jax.ShapeDtypeStruct((b, ATT_HEADS, t, HEAD_DIM), F32),
                     jax.ShapeDtypeStruct(kv_shape, F32), jax.ShapeDtypeStruct(kv_shape, F32)]
    else:
        out_specs = [pl.BlockSpec((None, tm, D_MODEL), lambda i, j: (i, j, 0))] * 3
        out_shape = [jax.ShapeDtypeStruct((b, t, D_MODEL), F32)] * 3
    return pl.pallas_call(
        functools.partial(_qkv_kernel, head_major=head_major),
        grid=(b, t // tm),
        in_specs=[pl.BlockSpec((None, tm, D_MODEL), lambda i, j: (i, j, 0)),
                  pl.BlockSpec((1, D_MODEL), fixed),
                  pl.BlockSpec(w_qkv.shape, fixed),
                  pl.BlockSpec((tm, LANES), lambda i, j: (j, 0)),
                  pl.BlockSpec((tm, LANES), lambda i, j: (j, 0))],
        out_specs=out_specs,
        out_shape=out_shape,
        scratch_shapes=[pltpu.VMEM(w_qkv.shape, BF16)],
        compiler_params=_cparams(2),
        name="qkv_rope",
    )(x, g.reshape(1, -1), w_qkv, cos_t, sin_t)


def _top_blocks(gate, valid, n_keep):
    nb = gate.shape[0]
    blk = lax.broadcasted_iota(jnp.int32, gate.shape, 0)
    rank = jnp.zeros(gate.shape, F32)
    for m in range(nb):
        g_m = gate[m:m + 1, :]
        beats = (g_m > gate) | ((g_m == gate) & (m < blk))
        rank = rank + jnp.where(beats & valid[m:m + 1, :], 1.0, 0.0)
    return jnp.where(valid & (rank < n_keep), 1.0, 0.0)


def _moba_prompt_kernel(q_ref, k_ref, v_ref, o_ref, *, t, heads_per_step):
    nb = t // MOBA_BLOCK
    neg = jnp.float32(-jnp.inf)
    outs = []
    for hh in range(heads_per_step):
        q = q_ref[hh]
        k = k_ref[:, hh].reshape(t, HEAD_DIM)
        v = v_ref[:, hh].reshape(t, HEAD_DIM)
        kmean = jnp.concatenate(
            [jnp.mean(k[n * MOBA_BLOCK:(n + 1) * MOBA_BLOCK, :], axis=0, keepdims=True) for n in range(nb)], axis=0)
        gate = lax.dot_general(kmean, q, (((1,), (1,)), ((), ())), precision=HIGHEST,
                               preferred_element_type=F32)
        own_of_q = lax.broadcasted_iota(jnp.int32, (nb, t), 1) // MOBA_BLOCK
        valid = lax.broadcasted_iota(jnp.int32, (nb, t), 0) < own_of_q
        sel_t = _top_blocks(gate, valid, MOBA_TOPK)
        pad_rows = 16 - nb
        sel_t = jnp.concatenate([sel_t, jnp.zeros((pad_rows, t), F32)], axis=0)
        eye = jnp.where(lax.broadcasted_iota(jnp.int32, (16, LANES), 0)
                        == lax.broadcasted_iota(jnp.int32, (16, LANES), 1), 1.0, 0.0)
        sel = _tdot(sel_t, eye)
        kb = k.astype(BF16)
        vb = jnp.concatenate([v, jnp.ones((t, LANES - HEAD_DIM), F32)], axis=1).astype(BF16)
        causal = (lax.broadcasted_iota(jnp.int32, (MOBA_BLOCK, MOBA_BLOCK), 1)
                  <= lax.broadcasted_iota(jnp.int32, (MOBA_BLOCK, MOBA_BLOCK), 0))
        o_blocks = []
        for own in range(nb):
            rows = slice(own * MOBA_BLOCK, (own + 1) * MOBA_BLOCK)
            qb = q[rows, :].astype(BF16)
            n_keys = (own + 1) * MOBA_BLOCK
            s_all = _bdot_t(qb, kb[:n_keys, :])
            scores = []
            for n in range(own):
                s = s_all[:, n * MOBA_BLOCK:(n + 1) * MOBA_BLOCK]
                scores.append(jnp.where(sel[rows, n:n + 1] > 0.0, s, neg))
            scores.append(jnp.where(causal, s_all[:, own * MOBA_BLOCK:], neg))
            m_elem = scores[0]
            for s in scores[1:]:
                m_elem = jnp.maximum(m_elem, s)
            m = m_elem.max(axis=-1, keepdims=True)
            p_all = jnp.concatenate([jnp.exp(s - m).astype(BF16) for s in scores], axis=1)
            acc = jnp.dot(p_all, vb[:n_keys, :], preferred_element_type=F32)
            o_blocks.append(acc[:, :HEAD_DIM] / acc[:, HEAD_DIM:HEAD_DIM + 1])
        outs.append(jnp.concatenate(o_blocks, axis=0))
    o_ref[...] = jnp.concatenate(outs, axis=1)


def _moba_prompt(q, k_rows, v_rows):
    b, n_h, t, _ = q.shape
    hps = LANES // HEAD_DIM
    npg = t // PAGE_SIZE
    kv_spec = pl.BlockSpec((None, npg, hps, PAGE_SIZE, HEAD_DIM), lambda i, j: (i, 0, j, 0, 0))
    return pl.pallas_call(
        functools.partial(_moba_prompt_kernel, t=t, heads_per_step=hps),
        grid=(b, n_h // hps),
        in_specs=[pl.BlockSpec((None, hps, t, HEAD_DIM), lambda i, j: (i, j, 0, 0)), kv_spec, kv_spec],
        out_specs=pl.BlockSpec((None, t, LANES), lambda i, j: (i, 0, j)),
        out_shape=jax.ShapeDtypeStruct((b, t, n_h * HEAD_DIM), F32),
        compiler_params=_cparams(2),
        name="moba_prompt",
    )(q, k_rows, v_rows)


def _kmean_select_kernel(pt_ref, q_ref, ptv_ref, *refs, pages_per_step, n_q, n_blk):
    page_refs = refs[:pages_per_step]
    sel_ref = refs[pages_per_step]
    kmt_ref = refs[pages_per_step + 1]
    s = pl.program_id(1)
    ppb = MOBA_BLOCK // PAGE_SIZE
    rows = ATT_HEADS * HEAD_DIM

    @pl.when(s == 0)
    def _():
        kmt_ref[...] = jnp.zeros_like(kmt_ref)

    lane = lax.broadcasted_iota(jnp.int32, (rows, LANES), 1)
    kmt = kmt_ref[...]
    for blk in range(pages_per_step // ppb):
        tot = page_refs[blk * ppb][...]
        for i in range(1, ppb):
            tot = tot + page_refs[blk * ppb + i][...]
        mean = tot.reshape(rows, PAGE_SIZE).sum(axis=-1, keepdims=True) * (1.0 / MOBA_BLOCK)
        kmt = jnp.where(lane == s * (pages_per_step // ppb) + blk, mean, kmt)
    kmt_ref[...] = kmt

    @pl.when(s == pl.num_programs(1) - 1)
    def _():
        q = q_ref[...]
        n_rows = q.shape[0]
        row_head = lax.broadcasted_iota(jnp.int32, (n_rows, LANES), 0) // n_q
        gate = jnp.zeros((n_rows, LANES), F32)
        for h in range(ATT_HEADS):
            g_h = jnp.dot(q, kmt[h * HEAD_DIM:(h + 1) * HEAD_DIM, :], precision=HIGHEST,
                          preferred_element_type=F32)
            gate = jnp.where(row_head == h, g_h, gate)
        blk_lane = lax.broadcasted_iota(jnp.int32, gate.shape, 1)
        neg = jnp.float32(-jnp.inf)
        gate = jnp.where(blk_lane < n_blk, gate, neg)
        pt_row = ptv_ref[...].astype(F32)
        pages = jnp.zeros(gate.shape, F32)
        for r in range(MOBA_TOPK):
            m = jnp.max(gate, axis=-1, keepdims=True)
            idx = jnp.min(jnp.where(gate == m, blk_lane, LANES), axis=-1, keepdims=True)
            for i in range(ppb):
                page = jnp.sum(jnp.where(blk_lane == idx * ppb + i, pt_row, 0.0), axis=-1, keepdims=True)
                pages = jnp.where(blk_lane == r * ppb + i, page, pages)
            gate = jnp.where(blk_lane == idx, neg, gate)
        sel_ref[...] = pages.astype(jnp.int32)


def _sample_select(q, cache_kt, page_table, n_blk, pages_per_step=32):
    db, n_h, n_q, _ = q.shape
    n_pages = page_table.shape[1]
    assert n_pages == LANES and n_blk <= LANES
    ppb = MOBA_BLOCK // PAGE_SIZE
    steps = n_blk * ppb // pages_per_step

    def page_spec(i):
        return pl.BlockSpec((None, n_h, HEAD_DIM, PAGE_SIZE),
                            lambda b, s, pt: (pt[b * n_pages + s * pages_per_step + i], 0, 0, 0))

    grid_spec = pltpu.PrefetchScalarGridSpec(
        num_scalar_prefetch=1,
        grid=(db, steps),
        in_specs=[pl.BlockSpec((None, n_h * n_q, HEAD_DIM), lambda b, s, pt: (b, 0, 0)),
                  pl.BlockSpec((None, 1, n_pages), lambda b, s, pt: (b, 0, 0))]
                 + [page_spec(i) for i in range(pages_per_step)],
        out_specs=pl.BlockSpec((None, n_h * n_q, LANES), lambda b, s, pt: (b, 0, 0)),
        scratch_shapes=[pltpu.VMEM((n_h * HEAD_DIM, LANES), F32)],
    )
    return pl.pallas_call(
        functools.partial(_kmean_select_kernel, pages_per_step=pages_per_step, n_q=n_q, n_blk=n_blk),
        grid_spec=grid_spec,
        out_shape=jax.ShapeDtypeStruct((db, n_h * n_q, LANES), jnp.int32),
        compiler_params=_cparams(2),
        name="sample_select",
    )(page_table.reshape(-1), q.reshape(db, n_h * n_q, HEAD_DIM), page_table.reshape(db, 1, n_pages),
      *([cache_kt] * pages_per_step))


def _sample_attn_kernel(pg_ref, q_ref, kn_ref, vn_ref, *refs, n_q, tiles_per_q):
    n_tiles = n_q * tiles_per_q
    k_refs = refs[:n_tiles]
    v_refs = refs[n_tiles:2 * n_tiles]
    o_ref = refs[2 * n_tiles]
    neg = jnp.float32(-jnp.inf)
    q = q_ref[...]
    kn, vn = kn_ref[...], vn_ref[...]
    rows8 = 8
    q8 = jnp.concatenate([q] * (rows8 // n_q), axis=0)
    kt_cat = jnp.concatenate([r[...] for r in k_refs], axis=1)
    vt_cat = jnp.concatenate([r[...] for r in v_refs], axis=1)
    s = _bdot(q8, kt_cat)
    row = lax.broadcasted_iota(jnp.int32, s.shape, 0) % n_q
    col = lax.broadcasted_iota(jnp.int32, s.shape, 1)
    s = jnp.where((col // (tiles_per_q * PAGE_SIZE)) == row, s, neg)
    row_o = lax.broadcasted_iota(jnp.int32, (rows8, 1), 0) % n_q
    s_own = [jnp.where(row_o >= c, jnp.sum(q8 * kn[c:c + 1, :], axis=-1, keepdims=True), neg)
             for c in range(n_q)]
    m = s.max(axis=-1, keepdims=True)
    for s_c in s_own:
        m = jnp.maximum(m, s_c)
    p = jnp.exp(s - m)
    den = p.sum(axis=-1, keepdims=True)
    acc = _bdot_t(p, vt_cat)
    for c, s_c in enumerate(s_own):
        p_c = jnp.exp(s_c - m)
        den = den + p_c
        acc = acc + p_c * vn[c:c + 1, :]
    o_ref[...] = (acc / den)[:n_q, :]


def _sample_attn(q, k_new, v_new, cache_kt, cache_vt, pages):
    db, n_h, n_q, _ = q.shape
    tiles_per_q = MOBA_TOPK * (MOBA_BLOCK // PAGE_SIZE)
    n_tiles = n_q * tiles_per_q

    def tile_spec(i):
        return pl.BlockSpec((None, None, HEAD_DIM, PAGE_SIZE),
                            lambda b, h, pg: (pg[(b * n_h + h) * n_tiles + i], h, 0, 0))

    small = pl.BlockSpec((None, None, n_q, HEAD_DIM), lambda b, h, pg: (b, h, 0, 0))
    grid_spec = pltpu.PrefetchScalarGridSpec(
        num_scalar_prefetch=1,
        grid=(db, n_h),
        in_specs=[small, small, small] + [tile_spec(i) for i in range(n_tiles)] * 2,
        out_specs=small,
    )
    return pl.pallas_call(
        functools.partial(_sample_attn_kernel, n_q=n_q, tiles_per_q=tiles_per_q),
        grid_spec=grid_spec,
        out_shape=jax.ShapeDtypeStruct((db, n_h, n_q, HEAD_DIM), F32),
        compiler_params=_cparams(2),
        name="sample_attn",
    )(pages, q, k_new, v_new, *([cache_kt] * n_tiles), *([cache_vt] * n_tiles))


def _even_layer(x, conv_prev, gla_prev, p, tm, conv_tt, gla_c, ffn_tm, ffn_tf):
    b, t, d = x.shape
    xf = x.reshape(b * t, d)
    glu, q, k, v, gate, la = _inproj(xf, p["norm_mix"], p["w_in"], p["w_a2"], p["b_a"], tm)
    as_bt = lambda a: a.reshape(b, t, a.shape[-1])
    conv_out, conv_state = _conv_module(as_bt(glu), conv_prev, p["conv_w"], p["conv_b"], p["ln_g"], p["ln_b"], conv_tt)
    o, gla_state = _gla(as_bt(q), as_bt(k), as_bt(la), as_bt(v), as_bt(gate),
                        gla_prev.reshape(b, GLA_HEADS * GLA_DK, GLA_DV), p["gla_norm_g"], gla_c, 4)
    h = _outproj(xf, [conv_out.reshape(b * t, -1), o.reshape(b * t, -1)], p["w_out"], tm)
    h = _ffn(h, p["norm_ffn"], p["ffn_w_gate"], p["ffn_w_up"], p["ffn_w_down"], ffn_tm, ffn_tf)
    return h.reshape(b, t, d), conv_state, gla_state.reshape(b, GLA_HEADS, GLA_DK, GLA_DV)


def kernel(x_prompt, x_sample, state_conv, state_gla, cache_k, cache_v, page_table, norm_mix_even, w_in_even, conv_w, conv_b, conv_ln_g, conv_ln_b, gla_w_a2, gla_b_a, gla_norm_g, w_out_even, norm_ffn_even, ffn_w_gate, ffn_w_up, ffn_w_down, norm_mix_odd, w_qkv_odd, w_out_odd, norm_ffn_odd, router_w, moe_w_gate, moe_w_up, moe_w_down, final_norm_g):
    bp, tp, d = x_prompt.shape
    db, ts, _ = x_sample.shape
    even = dict(norm_mix=norm_mix_even[0], w_in=w_in_even[0], conv_w=conv_w[0], conv_b=conv_b[0],
                ln_g=conv_ln_g[0], ln_b=conv_ln_b[0], w_a2=gla_w_a2[0], b_a=gla_b_a[0],
                gla_norm_g=gla_norm_g[0], w_out=w_out_even[0], norm_ffn=norm_ffn_even[0],
                ffn_w_gate=ffn_w_gate, ffn_w_up=ffn_w_up, ffn_w_down=ffn_w_down)

    hp, conv_p, gla_p = _even_layer(
        x_prompt, jnp.zeros((bp, CONV_W - 1, D_CONV), F32), jnp.zeros((bp, GLA_HEADS, GLA_DK, GLA_DV), F32),
        even, tm=512, conv_tt=256, gla_c=GLA_CHUNK, ffn_tm=1024, ffn_tf=256)

    m_s = db * ts
    xs = x_sample.reshape(m_s, d)
    glu, q, k, v, gate, la = _inproj(xs, even["norm_mix"], even["w_in"], even["w_a2"], even["b_a"], m_s)
    as_bt = lambda a: a.reshape(db, ts, a.shape[-1])
    conv_out_s, conv_s = _conv_module(as_bt(glu), state_conv[0], even["conv_w"], even["conv_b"],
                                      even["ln_g"], even["ln_b"], ts)
    pad_t = 16
    pad8 = lambda a: jnp.pad(as_bt(a), ((0, 0), (0, pad_t - ts), (0, 0)))
    o_s, gla_s = _gla(pad8(q), pad8(k), pad8(la), pad8(v), pad8(gate),
                      state_gla[0].reshape(db, GLA_HEADS * GLA_DK, GLA_DV), even["gla_norm_g"], pad_t, 4)
    hs = _outproj(xs, [conv_out_s.reshape(m_s, -1), o_s[:, :ts].reshape(m_s, -1)], even["w_out"], m_s)
    hs = _ffn(hs, even["norm_ffn"], ffn_w_gate, ffn_w_up, ffn_w_down, m_s, 256)
    gla_s = gla_s.reshape(db, GLA_HEADS, GLA_DK, GLA_DV)

    qp, k_rows, v_rows = _qkv_rope(hp, norm_mix_odd[0], w_qkv_odd[0], jnp.arange(tp), 256, True)
    op = _moba_prompt(qp, k_rows, v_rows)
    hp2, xn_p, gate_p, dest_p = _route(hp.reshape(bp * tp, d), op.reshape(bp * tp, d), w_out_odd[0],
                                       norm_ffn_odd[0], router_w[0], 1024)

    n_blocks = page_table.shape[1] * PAGE_SIZE // MOBA_BLOCK
    past_len = page_table.shape[1] * PAGE_SIZE
    pos_s = past_len + (jnp.arange(m_s) % ts)
    qs, ks, vs = _qkv_rope(hs.reshape(1, m_s, d), norm_mix_odd[0], w_qkv_odd[0], pos_s, m_s, False)
    to_heads = lambda a: a.reshape(db, ts, ATT_HEADS, HEAD_DIM).transpose(0, 2, 1, 3)
    qs, ks, vs = to_heads(qs), to_heads(ks), to_heads(vs)
    cache_kt, cache_vt = jnp.swapaxes(cache_k[0], -1, -2), jnp.swapaxes(cache_v[0], -1, -2)
    tiles_per_q = MOBA_TOPK * (MOBA_BLOCK // PAGE_SIZE)
    pages = _sample_select(qs, cache_kt, page_table, n_blocks)[:, :, :tiles_per_q].reshape(-1)
    os_ = _sample_attn(qs, ks, vs, cache_kt, cache_vt, pages)
    os_ = os_.transpose(0, 2, 1, 3).reshape(m_s, d)
    hs2 = _outproj(hs, [os_], w_out_odd[0], m_s)
    y_sample, moe_wg, moe_wu, moe_wd = _ffn(hs2, norm_ffn_odd[0], moe_w_gate[0], moe_w_up[0], moe_w_down[0], m_s, 512,
                                            router_w=router_w[0], final_g=final_norm_g, emit_bf16=True)
    y_sample = y_sample.reshape(db, ts, d)

    y_prompt = _experts(hp2, xn_p, gate_p, dest_p, moe_wg, moe_wu, moe_wd, final_norm_g,
                        tm=1024, tf=1792, rb=128, mb=256).reshape(bp, tp, d)

    return (y_prompt, y_sample, conv_p[None], gla_p[None], k_rows[None], v_rows[None],
            conv_s[None], gla_s[None], ks[None], vs[None])
```

```python
import functools

import jax
import jax.numpy as jnp
import numpy as np
from jax import lax
from jax.experimental import pallas as pl
from jax.experimental.pallas import tpu as pltpu

F32 = jnp.float32
BF16 = jnp.bfloat16
HIGHEST = lax.Precision.HIGHEST

LANES = 128
SUBLANES = 8
NORM_EPS = 1e-6
D_MODEL = 1024
D_CONV = 512
CONV_W = 31
CONV_HALO = 32
GLA_HEADS = 4
GLA_DK = 64
GLA_DV = 128
GLA_TAU = 16.0
GLA_CHUNK = 64
ATT_HEADS = 16
HEAD_DIM = 64
MOBA_BLOCK = 256
MOBA_TOPK = 3
PAGE_SIZE = 128
ROPE_THETA = 10000.0
N_EXPERTS = 8
VMEM_LIMIT = 60 * 1024 * 1024


def _cparams(n_axes):
    return pltpu.CompilerParams(dimension_semantics=("arbitrary",) * n_axes,
                                vmem_limit_bytes=VMEM_LIMIT)


def _rms(x, g):
    return x * lax.rsqrt(jnp.mean(x * x, axis=-1, keepdims=True) + NORM_EPS) * g


def _silu(x):
    return x * jax.nn.sigmoid(x)


def _bdot(a, b):
    return jnp.dot(a.astype(BF16), b.astype(BF16), preferred_element_type=F32)


def _bdot_t(a, b):
    return lax.dot_general(a.astype(BF16), b.astype(BF16), (((1,), (1,)), ((), ())),
                           preferred_element_type=F32)


def _dot_split(a, b):
    a_hi, b_hi = a.astype(BF16), b.astype(BF16)
    a_lo = (a - a_hi.astype(F32)).astype(BF16)
    b_lo = (b - b_hi.astype(F32)).astype(BF16)
    dot = functools.partial(jnp.dot, preferred_element_type=F32)
    return dot(a_hi, b_hi) + (dot(a_hi, b_lo) + dot(a_lo, b_hi))


def _tdot(a, b):
    return lax.dot_general(a.astype(BF16), b.astype(BF16), (((0,), (0,)), ((), ())),
                           preferred_element_type=F32)


def _inproj_kernel(x_ref, g_ref, w_ref, wa1_ref, wa2_ref, ba_ref,
                   glu_ref, q_ref, k_ref, v_ref, gate_ref, la_ref, wbf_ref, wa1bf_ref):
    @pl.when(pl.program_id(0) == 0)
    def _():
        wbf_ref[...] = w_ref[...].astype(BF16)
        wa1bf_ref[...] = wa1_ref[...].astype(BF16)

    xn = _rms(x_ref[...], g_ref[...]).astype(BF16)
    y = jnp.dot(xn, wbf_ref[...], preferred_element_type=F32)
    glu_ref[...] = y[:, :D_CONV] * jax.nn.sigmoid(y[:, D_CONV:2 * D_CONV])
    q_ref[...] = y[:, 1024:1280] * (GLA_DK ** -0.5)
    k_ref[...] = y[:, 1280:1536]
    v_ref[...] = y[:, 1536:2048]
    gate_ref[...] = y[:, 2048:2560]
    a = jnp.dot(xn, wa1bf_ref[...], preferred_element_type=F32)
    z = jnp.dot(a, wa2_ref[...], precision=HIGHEST, preferred_element_type=F32) + ba_ref[...]
    log_sig = jnp.minimum(z, 0.0) - jnp.log(1.0 + jnp.exp(-jnp.abs(z)))
    la_ref[...] = log_sig * (1.0 / GLA_TAU)


def _inproj(x, g, w_in, w_a2, b_a, tm):
    m = x.shape[0]
    n_main = 2560
    wa1 = jnp.pad(w_in[:, n_main:], ((0, 0), (0, LANES - (w_in.shape[1] - n_main))))
    wa2 = jnp.pad(w_a2, ((0, LANES - w_a2.shape[0]), (0, 0)))
    row = lambda i: (i, 0)
    fixed = lambda i: (0, 0)
    outs = [(D_CONV, F32), (256, F32), (256, F32), (512, F32), (512, F32), (256, F32)]
    return pl.pallas_call(
        _inproj_kernel,
        grid=(m // tm,),
        in_specs=[pl.BlockSpec((tm, D_MODEL), row),
                  pl.BlockSpec((1, D_MODEL), fixed),
                  pl.BlockSpec((D_MODEL, n_main), fixed),
                  pl.BlockSpec((D_MODEL, LANES), fixed),
                  pl.BlockSpec((LANES, 256), fixed),
                  pl.BlockSpec((1, 256), fixed)],
        out_specs=[pl.BlockSpec((tm, n), row) for n, _ in outs],
        out_shape=[jax.ShapeDtypeStruct((m, n), dt) for n, dt in outs],
        scratch_shapes=[pltpu.VMEM((D_MODEL, n_main), BF16), pltpu.VMEM((D_MODEL, LANES), BF16)],
        compiler_params=_cparams(1),
        name="inproj",
    )(x, g.reshape(1, -1), w_in, wa1, wa2, b_a.reshape(1, -1))


def _conv_kernel(glu_ref, prev_ref, w_ref, b_ref, lg_ref, lb_ref, out_ref, st_ref, full_ref, ph_ref, *, tt, rc):
    t = pl.program_id(1)
    lo = CONV_HALO - (CONV_W - 1)

    @pl.when(t == 0)
    def _():
        full_ref[0:lo, :] = jnp.zeros((lo, D_CONV), F32)
        full_ref[lo:CONV_HALO, :] = prev_ref[...]

    full_ref[CONV_HALO:CONV_HALO + tt, :] = glu_ref[...]
    span = ph_ref.shape[1]
    for r in range(1, SUBLANES):
        ph_ref[r - 1] = full_ref[r:r + span, :]

    def slab(c, j):
        a, r = divmod(lo + j, SUBLANES)
        start = c * rc + a * SUBLANES
        return full_ref[start:start + rc, :] if r == 0 else ph_ref[r - 1, start:start + rc, :]

    for c in range(tt // rc):
        acc = slab(c, 0) * w_ref[0:1, :]
        for j in range(1, CONV_W):
            acc = acc + slab(c, j) * w_ref[j:j + 1, :]
        y = acc + b_ref[...]
        mu = jnp.mean(y, axis=-1, keepdims=True)
        d = y - mu
        var = jnp.mean(d * d, axis=-1, keepdims=True)
        yn = d * lax.rsqrt(var + NORM_EPS) * lg_ref[...] + lb_ref[...]
        out_ref[c * rc:(c + 1) * rc, :] = _silu(yn).astype(out_ref.dtype)

    tail = full_ref[tt + lo:tt + CONV_HALO, :]

    @pl.when(t == pl.num_programs(1) - 1)
    def _():
        st_ref[...] = tail

    full_ref[lo:CONV_HALO, :] = tail


def _conv_module(glu, prev, conv_w, conv_b, ln_g, ln_b, tt):
    b, t, _ = glu.shape
    rc = min(tt, 32)
    fixed = lambda i, j: (0, 0)
    return pl.pallas_call(
        functools.partial(_conv_kernel, tt=tt, rc=rc),
        grid=(b, t // tt),
        in_specs=[pl.BlockSpec((None, tt, D_CONV), lambda i, j: (i, j, 0)),
                  pl.BlockSpec((None, CONV_W - 1, D_CONV), lambda i, j: (i, 0, 0)),
                  pl.BlockSpec((CONV_W, D_CONV), fixed),
                  pl.BlockSpec((1, D_CONV), fixed),
                  pl.BlockSpec((1, D_CONV), fixed),
                  pl.BlockSpec((1, D_CONV), fixed)],
        out_specs=[pl.BlockSpec((None, tt, D_CONV), lambda i, j: (i, j, 0)),
                   pl.BlockSpec((None, CONV_W - 1, D_CONV), lambda i, j: (i, 0, 0))],
        out_shape=[jax.ShapeDtypeStruct((b, t, D_CONV), BF16),
                   jax.ShapeDtypeStruct((b, CONV_W - 1, D_CONV), F32)],
        scratch_shapes=[pltpu.VMEM((CONV_HALO + tt, D_CONV), F32),
                        pltpu.VMEM((SUBLANES - 1, CONV_HALO + tt - SUBLANES, D_CONV), F32)],
        compiler_params=_cparams(2),
        name="conv_module",
    )(glu, prev, conv_w, conv_b.reshape(1, -1), ln_g.reshape(1, -1), ln_b.reshape(1, -1))


def _gla_levels(c):
    levels, g_big = [], c
    while g_big > 1:
        g_small = max(g_big // 4, 1)
        levels.append((g_big, g_small))
        g_big = g_small
    return levels


def _gla_decay_matrices(c):
    t = np.arange(c)[:, None]
    s = np.arange(c)[None, :]
    mats = [(s <= t), (s > t)]
    for g_big, g_small in _gla_levels(c):
        sub = (t % g_big) // g_small
        ref_q = (t // g_big) * g_big + sub * g_small - 1
        mats.append((sub >= 1) & (s > ref_q) & (s <= t))
        for i in range(1, g_big // g_small):
            ref_k = (t // g_big) * g_big + i * g_small - 1
            mats.append(((t % g_big) < i * g_small) & (s > t) & (s <= ref_k))
    return np.concatenate([m.astype(np.float32) for m in mats], axis=0)


def _gla_kernel(q_ref, k_ref, la_ref, v_ref, gate_ref, s0_ref, ng_ref, dmat_ref,
                out_ref, st_ref, s_ref, *, c):
    ci = pl.program_id(1)

    @pl.when(ci == 0)
    def _():
        s_ref[...] = s0_ref[...]

    for bi in range(q_ref.shape[0]):
        out, new_state = _gla_chunk(q_ref[bi], k_ref[bi], la_ref[bi], v_ref[bi], gate_ref[bi], s_ref[bi],
                                    ng_ref[...], dmat_ref[...], c)
        out_ref[bi] = out.astype(out_ref.dtype)
        s_ref[bi] = new_state

    @pl.when(ci == pl.num_programs(1) - 1)
    def _():
        st_ref[...] = s_ref[...]


def _gla_chunk(q, k, la, v, gate, state, norm_g, dmat, c):
    hk = GLA_HEADS * GLA_DK
    la1 = la.astype(BF16)
    la2 = (la - la1.astype(F32)).astype(BF16)
    e_all = (jnp.dot(dmat, la1, preferred_element_type=F32)
             + jnp.dot(dmat, la2, preferred_element_type=F32))

    def e_blk(i):
        return e_all[i * c:(i + 1) * c, :]

    lane_head = lax.broadcasted_iota(jnp.int32, (c, hk), 1) // GLA_DK
    head_masks = [(lane_head == h).astype(F32) for h in range(GLA_HEADS)]

    def stack_heads(x):
        return jnp.concatenate([x * m for m in head_masks], axis=0).astype(BF16)

    row_t = lax.broadcasted_iota(jnp.int32, (GLA_HEADS * c, c), 0) % c
    col_s = lax.broadcasted_iota(jnp.int32, (GLA_HEADS * c, c), 1)
    row_k = lax.broadcasted_iota(jnp.int32, (c, hk), 0)

    o_st = jnp.dot(stack_heads(q * jnp.exp(e_blk(0))), state.astype(BF16),
                   preferred_element_type=F32)

    att = jnp.where(row_t == col_s, _bdot_t(stack_heads(q), k), 0.0)
    idx = 2
    for g_big, g_small in _gla_levels(c):
        q_lvl = stack_heads(q * jnp.exp(e_blk(idx)))
        idx += 1
        n_var = g_big // g_small - 1
        k_vars = []
        for i in range(1, n_var + 1):
            k_vars.append(jnp.where((row_k % g_big) < i * g_small, k * jnp.exp(e_blk(idx)), 0.0))
            idx += 1
        p = _bdot_t(q_lvl, jnp.concatenate(k_vars, axis=0))
        wide_t = lax.broadcasted_iota(jnp.int32, p.shape, 0) % c
        wide_col = lax.broadcasted_iota(jnp.int32, p.shape, 1)
        keep = ((wide_t // g_big) == ((wide_col % c) // g_big)) & ((wide_t % g_big) // g_small == wide_col // c + 1)
        p = jnp.where(keep, p, 0.0)
        for i in range(n_var):
            att = att + p[:, i * c:(i + 1) * c]

    att = att.astype(BF16)
    outs = []
    for h in range(GLA_HEADS):
        v_h = v[:, h * GLA_DV:(h + 1) * GLA_DV]
        o_h = o_st[h * c:(h + 1) * c, :] + jnp.dot(att[h * c:(h + 1) * c, :], v_h.astype(BF16),
                                                   preferred_element_type=F32)
        o_h = _rms(o_h, norm_g[:, h * GLA_DV:(h + 1) * GLA_DV])
        outs.append(o_h * _silu(gate[:, h * GLA_DV:(h + 1) * GLA_DV]))

    kd = (k * jnp.exp(e_blk(1))).astype(BF16)
    ones = jnp.ones((c, GLA_DV), BF16)
    b_last = _tdot(la1, ones) + _tdot(la2, ones)
    row_head = lax.broadcasted_iota(jnp.int32, (hk, GLA_DV), 0) // GLA_DK
    upd = jnp.zeros((hk, GLA_DV), F32)
    for h in range(GLA_HEADS):
        kv = _tdot(kd, v[:, h * GLA_DV:(h + 1) * GLA_DV])
        upd = upd + jnp.where(row_head == h, kv, 0.0)
    return jnp.concatenate(outs, axis=1), jnp.exp(b_last) * state + upd


def _gla(q, k, la, v, gate, s0, norm_g, c, nbs):
    b, t, hk = q.shape
    assert b % nbs == 0 and t % c == 0
    dv_all = v.shape[-1]
    dmat = jnp.asarray(_gla_decay_matrices(c), dtype=BF16)
    tok = lambda n: pl.BlockSpec((nbs, c, n), lambda i, j: (i, j, 0))
    fixed = lambda i, j: (0, 0)
    return pl.pallas_call(
        functools.partial(_gla_kernel, c=c),
        grid=(b // nbs, t // c),
        in_specs=[tok(hk), tok(hk), tok(hk), tok(dv_all), tok(dv_all),
                  pl.BlockSpec((nbs, hk, GLA_DV), lambda i, j: (i, 0, 0)),
                  pl.BlockSpec((1, dv_all), fixed),
                  pl.BlockSpec(dmat.shape, fixed)],
        out_specs=[tok(dv_all), pl.BlockSpec((nbs, hk, GLA_DV), lambda i, j: (i, 0, 0))],
        out_shape=[jax.ShapeDtypeStruct((b, t, dv_all), BF16),
                   jax.ShapeDtypeStruct((b, hk, GLA_DV), F32)],
        scratch_shapes=[pltpu.VMEM((nbs, hk, GLA_DV), F32)],
        compiler_params=_cparams(2),
        name="gla",
    )(q, k, la, v, gate, s0, norm_g.reshape(1, -1), dmat)


def _outproj_kernel(*refs, n_parts):
    x_ref = refs[0]
    a_refs = refs[1:1 + n_parts]
    w_ref = refs[1 + n_parts]
    out_ref = refs[2 + n_parts]
    wbf_ref = refs[3 + n_parts]

    @pl.when(pl.program_id(0) == 0)
    def _():
        wbf_ref[...] = w_ref[...].astype(BF16)

    acc = x_ref[...]
    off = 0
    for a_ref in a_refs:
        n = a_ref.shape[-1]
        acc = acc + jnp.dot(a_ref[...].astype(BF16), wbf_ref[off:off + n, :], preferred_element_type=F32)
        off += n
    out_ref[...] = acc


def _outproj(x, parts, w, tm):
    m = x.shape[0]
    row = lambda i: (i, 0)
    return pl.pallas_call(
        functools.partial(_outproj_kernel, n_parts=len(parts)),
        grid=(m // tm,),
        in_specs=[pl.BlockSpec((tm, D_MODEL), row)]
                 + [pl.BlockSpec((tm, p.shape[-1]), row) for p in parts]
                 + [pl.BlockSpec(w.shape, lambda i: (0, 0))],
        out_specs=pl.BlockSpec((tm, D_MODEL), row),
        out_shape=jax.ShapeDtypeStruct((m, D_MODEL), F32),
        scratch_shapes=[pltpu.VMEM(w.shape, BF16)],
        compiler_params=_cparams(1),
        name="outproj",
    )(x, *parts, w)


def _ffn_kernel(*refs, routed, final_norm, emit_bf16):
    it = iter(refs)
    h_ref, g_ref = next(it), next(it)
    rw_ref = next(it) if routed else None
    wg_ref, wu_ref, wd_ref = next(it), next(it), next(it)
    fg_ref = next(it) if final_norm else None
    out_ref = next(it)
    bf_refs = [next(it), next(it), next(it)] if emit_bf16 else None
    xn_ref, acc_ref = next(it), next(it)
    gate_ref = next(it) if routed else None
    e, f = pl.program_id(1), pl.program_id(2)

    @pl.when((e == 0) & (f == 0))
    def _():
        xn = _rms(h_ref[...], g_ref[...])
        xn_ref[...] = xn.astype(BF16)
        acc_ref[...] = jnp.zeros_like(acc_ref)
        if routed:
            gate_ref[...] = _top2_gates(jnp.dot(xn, rw_ref[...], precision=HIGHEST, preferred_element_type=F32))

    xn = xn_ref[...]
    wg, wu, wd = wg_ref[...].astype(BF16), wu_ref[...].astype(BF16), wd_ref[...].astype(BF16)
    if emit_bf16:
        for ref, w in zip(bf_refs, (wg, wu, wd)):
            ref[...] = w
    hid = _silu(jnp.dot(xn, wg, preferred_element_type=F32)) * jnp.dot(xn, wu, preferred_element_type=F32)
    if routed:
        lane = lax.broadcasted_iota(jnp.int32, gate_ref.shape, 1)
        hid = hid * jnp.sum(jnp.where(lane == e, gate_ref[...], 0.0), axis=-1, keepdims=True)
    acc_ref[...] += jnp.dot(hid.astype(BF16), wd, preferred_element_type=F32)

    @pl.when((e == pl.num_programs(1) - 1) & (f == pl.num_programs(2) - 1))
    def _():
        y = h_ref[...] + acc_ref[...]
        if final_norm:
            y = _rms(y, fg_ref[...])
        out_ref[...] = y


def _ffn(h, g, w_gate, w_up, w_down, tm, tf, router_w=None, final_g=None, emit_bf16=False):
    m = h.shape[0]
    n_e, _, d_ff = w_gate.shape
    assert not emit_bf16 or m == tm
    routed = router_w is not None
    final_norm = final_g is not None
    row = lambda i, e, f: (i, 0)
    fixed = lambda i, e, f: (0, 0)
    args = [h, g.reshape(1, -1)]
    in_specs = [pl.BlockSpec((tm, D_MODEL), row), pl.BlockSpec((1, D_MODEL), fixed)]
    if routed:
        args.append(jnp.pad(router_w, ((0, 0), (0, LANES - router_w.shape[1]))))
        in_specs.append(pl.BlockSpec((D_MODEL, LANES), fixed))
    args += [w_gate, w_up, w_down]
    w_specs = [pl.BlockSpec((None, D_MODEL, tf), lambda i, e, f: (e, 0, f)),
               pl.BlockSpec((None, D_MODEL, tf), lambda i, e, f: (e, 0, f)),
               pl.BlockSpec((None, tf, D_MODEL), lambda i, e, f: (e, f, 0))]
    in_specs += w_specs
    if final_norm:
        args.append(final_g.reshape(1, -1))
        in_specs.append(pl.BlockSpec((1, D_MODEL), fixed))
    scratch = [pltpu.VMEM((tm, D_MODEL), BF16), pltpu.VMEM((tm, D_MODEL), F32)]
    if routed:
        scratch.append(pltpu.VMEM((tm, LANES), F32))
    out_specs = [pl.BlockSpec((tm, D_MODEL), row)]
    out_shape = [jax.ShapeDtypeStruct((m, D_MODEL), F32)]
    if emit_bf16:
        out_specs += w_specs
        out_shape += [jax.ShapeDtypeStruct(w.shape, BF16) for w in (w_gate, w_up, w_down)]
    outs = pl.pallas_call(
        functools.partial(_ffn_kernel, routed=routed, final_norm=final_norm, emit_bf16=emit_bf16),
        grid=(m // tm, n_e, d_ff // tf),
        in_specs=in_specs,
        out_specs=out_specs,
        out_shape=out_shape,
        scratch_shapes=scratch,
        compiler_params=_cparams(3),
        name="moe_ffn" if routed else "ffn",
    )(*args)
    return outs if emit_bf16 else outs[0]


def _top2_gates(logits):
    lane = lax.broadcasted_iota(jnp.int32, logits.shape, 1)
    neg = jnp.float32(-jnp.inf)
    logits = jnp.where(lane < N_EXPERTS, logits, neg)
    m1 = jnp.max(logits, axis=-1, keepdims=True)
    i1 = jnp.min(jnp.where(logits == m1, lane, LANES), axis=-1, keepdims=True)
    rest = jnp.where(lane == i1, neg, logits)
    m2 = jnp.max(rest, axis=-1, keepdims=True)
    i2 = jnp.min(jnp.where(rest == m2, lane, LANES), axis=-1, keepdims=True)
    e2 = jnp.exp(m2 - m1)
    g1 = 1.0 / (1.0 + e2)
    g2 = e2 / (1.0 + e2)
    return jnp.where(lane == i1, g1, 0.0) + jnp.where(lane == i2, g2, 0.0)


def _route_kernel(x_ref, a_ref, w_ref, g_ref, rw_ref, h_ref, xn_ref, gate_ref, dest_ref, wbf_ref, ltri_ref):
    tm = x_ref.shape[0]

    @pl.when(pl.program_id(0) == 0)
    def _():
        wbf_ref[...] = w_ref[...].astype(BF16)
        earlier = (lax.broadcasted_iota(jnp.int32, (tm, tm), 1) < lax.broadcasted_iota(jnp.int32, (tm, tm), 0))
        ltri_ref[...] = jnp.where(earlier, 1.0, 0.0).astype(BF16)

    h = x_ref[...] + jnp.dot(a_ref[...].astype(BF16), wbf_ref[...], preferred_element_type=F32)
    h_ref[...] = h
    xn = _rms(h, g_ref[...])
    xn_ref[...] = xn.astype(BF16)
    gate = _top2_gates(_dot_split(xn, rw_ref[...]))
    gate_ref[...] = gate
    routed = gate > 0.0
    rank = jnp.dot(ltri_ref[...], jnp.where(routed, 1.0, 0.0).astype(BF16), preferred_element_type=F32)
    dest_ref[...] = jnp.where(routed, rank, -1.0)


def _route(x, a, w_out, g, router_w, tm):
    m = x.shape[0]
    row = lambda i: (i, 0)
    fixed = lambda i: (0, 0)
    rw = jnp.pad(router_w, ((0, 0), (0, LANES - router_w.shape[1])))
    return pl.pallas_call(
        _route_kernel,
        grid=(m // tm,),
        in_specs=[pl.BlockSpec((tm, D_MODEL), row), pl.BlockSpec((tm, D_MODEL), row),
                  pl.BlockSpec(w_out.shape, fixed), pl.BlockSpec((1, D_MODEL), fixed),
                  pl.BlockSpec((D_MODEL, LANES), fixed)],
        out_specs=[pl.BlockSpec((tm, D_MODEL), row), pl.BlockSpec((tm, D_MODEL), row),
                   pl.BlockSpec((tm, LANES), row), pl.BlockSpec((tm, LANES), row)],
        out_shape=[jax.ShapeDtypeStruct((m, D_MODEL), F32), jax.ShapeDtypeStruct((m, D_MODEL), BF16),
                   jax.ShapeDtypeStruct((m, LANES), F32), jax.ShapeDtypeStruct((m, LANES), F32)],
        scratch_shapes=[pltpu.VMEM(w_out.shape, BF16), pltpu.VMEM((tm, tm), BF16)],
        compiler_params=_cparams(1),
        name="route",
    )(x, a, w_out, g.reshape(1, -1), rw)


def _experts_kernel(cnt_ref, xn_ref, gate_ref, dest_ref, h_ref, wg_ref, wu_ref, wd_ref, fg_ref, out_ref,
                    dest_t_ref, xc_ref, yc_ref, *, rb, mb):
    i, e, f = pl.program_id(0), pl.program_id(1), pl.program_id(2)
    n_e, n_f = pl.num_programs(1), pl.num_programs(2)
    tm = xn_ref.shape[0]

    @pl.when((e == 0) & (f == 0))
    def _():
        dest_t_ref[...] = jnp.transpose(dest_ref[...])
        out_ref[...] = h_ref[...]

    count = cnt_ref[i * n_e + e]
    n_blk = (count + rb - 1) // rb
    slot_row = dest_t_ref[pl.ds(e, 1), :]

    def rows(blk):
        return pl.ds(pl.multiple_of(blk * rb, rb), rb)

    n_mv = (count + mb - 1) // mb

    def mrows(blk):
        return pl.ds(pl.multiple_of(blk * mb, mb), mb)

    @pl.when(f == 0)
    def _():
        def gather(blk, carry):
            want = (blk * mb + lax.broadcasted_iota(jnp.int32, (mb, tm), 0)).astype(F32)
            pick = jnp.where(slot_row == want, 1.0, 0.0).astype(BF16)
            xc_ref[mrows(blk), :] = jnp.dot(pick, xn_ref[...], preferred_element_type=F32).astype(BF16)
            yc_ref[mrows(blk), :] = jnp.zeros((mb, D_MODEL), F32)
            return carry
        lax.fori_loop(0, n_mv, gather, 0)

    def expert(blk, carry):
        xc = xc_ref[rows(blk), :]
        hid = _silu(jnp.dot(xc, wg_ref[...], preferred_element_type=F32)) \
            * jnp.dot(xc, wu_ref[...], preferred_element_type=F32)
        yc_ref[rows(blk), :] += jnp.dot(hid.astype(BF16), wd_ref[...], preferred_element_type=F32)
        return carry
    lax.fori_loop(0, n_blk, expert, 0)

    @pl.when(f == n_f - 1)
    def _():
        lane = lax.broadcasted_iota(jnp.int32, (tm, LANES), 1)
        slot_col = jnp.sum(jnp.where(lane == e, dest_ref[...], 0.0), axis=-1, keepdims=True)
        gate_col = jnp.sum(jnp.where(lane == e, gate_ref[...], 0.0), axis=-1, keepdims=True)

        def scatter(blk, carry):
            want = (blk * mb + lax.broadcasted_iota(jnp.int32, (tm, mb), 1)).astype(F32)
            place = jnp.where(slot_col == want, 1.0, 0.0).astype(BF16)
            out_ref[...] += gate_col * jnp.dot(place, yc_ref[mrows(blk), :].astype(BF16),
                                               preferred_element_type=F32)
            return carry
        lax.fori_loop(0, n_mv, scatter, 0)

    @pl.when((e == n_e - 1) & (f == n_f - 1))
    def _():
        out_ref[...] = _rms(out_ref[...], fg_ref[...])


def _experts(h, xn, gate, dest, w_gate, w_up, w_down, final_g, tm, tf, rb, mb):
    m = h.shape[0]
    n_e, _, d_ff = w_gate.shape
    assert tm % mb == 0 and mb % rb == 0 and m % tm == 0 and d_ff % tf == 0
    counts = jnp.sum((dest[:, :n_e] >= 0.0).reshape(m // tm, tm, n_e), axis=1, dtype=jnp.int32).reshape(-1)
    row = lambda i, e, f, c: (i, 0)
    grid_spec = pltpu.PrefetchScalarGridSpec(
        num_scalar_prefetch=1,
        grid=(m // tm, n_e, d_ff // tf),
        in_specs=[pl.BlockSpec((tm, D_MODEL), row), pl.BlockSpec((tm, LANES), row), pl.BlockSpec((tm, LANES), row),
                  pl.BlockSpec((tm, D_MODEL), row),
                  pl.BlockSpec((None, D_MODEL, tf), lambda i, e, f, c: (e, 0, f)),
                  pl.BlockSpec((None, D_MODEL, tf), lambda i, e, f, c: (e, 0, f)),
                  pl.BlockSpec((None, tf, D_MODEL), lambda i, e, f, c: (e, f, 0)),
                  pl.BlockSpec((1, D_MODEL), lambda i, e, f, c: (0, 0))],
        out_specs=pl.BlockSpec((tm, D_MODEL), row),
        scratch_shapes=[pltpu.VMEM((LANES, tm), F32), pltpu.VMEM((tm, D_MODEL), BF16),
                        pltpu.VMEM((tm, D_MODEL), F32)],
    )
    return pl.pallas_call(
        functools.partial(_experts_kernel, rb=rb, mb=mb),
        grid_spec=grid_spec,
        out_shape=jax.ShapeDtypeStruct((m, D_MODEL), F32),
        compiler_params=_cparams(3),
        name="experts",
    )(counts, xn, gate, dest, h, w_gate, w_up, w_down, final_g.reshape(1, -1))


def _rope_tables(pos):
    half = HEAD_DIM // 2
    inv = jnp.power(ROPE_THETA, -jnp.arange(half, dtype=F32) / half)
    ang = pos.astype(F32)[:, None] * inv[None, :]
    cos, sin = jnp.cos(ang), jnp.sin(ang)
    cos_t = jnp.concatenate([cos, cos, cos, cos], axis=1)
    sin_t = jnp.concatenate([-sin, sin, -sin, sin], axis=1)
    return cos_t, sin_t


def _rope_apply(y, cos_t, sin_t):
    half = HEAD_DIM // 2
    first = (lax.broadcasted_iota(jnp.int32, (y.shape[0], LANES), 1) % HEAD_DIM) < half
    cols = []
    for c in range(y.shape[1] // LANES):
        yc = y[:, c * LANES:(c + 1) * LANES]
        swapped = jnp.where(first, pltpu.roll(yc, LANES - half, 1), pltpu.roll(yc, half, 1))
        cols.append(yc * cos_t + swapped * sin_t)
    return jnp.concatenate(cols, axis=1)


def _qkv_kernel(x_ref, g_ref, w_ref, cos_ref, sin_ref, q_ref, k_ref, v_ref, wbf_ref, *, head_major):
    @pl.when((pl.program_id(0) == 0) & (pl.program_id(1) == 0))
    def _():
        wbf_ref[...] = w_ref[...].astype(BF16)

    xn = _rms(x_ref[...], g_ref[...]).astype(BF16)
    y = jnp.dot(xn, wbf_ref[...], preferred_element_type=F32)
    cos_t, sin_t = cos_ref[...], sin_ref[...]
    q = _rope_apply(y[:, :D_MODEL], cos_t, sin_t) * (HEAD_DIM ** -0.5)
    k = _rope_apply(y[:, D_MODEL:2 * D_MODEL], cos_t, sin_t)
    v = y[:, 2 * D_MODEL:]
    if head_major:
        tm = q.shape[0]
        for h in range(ATT_HEADS):
            sl = slice(h * HEAD_DIM, (h + 1) * HEAD_DIM)
            q_ref[h] = q[:, sl]
            for p in range(tm // PAGE_SIZE):
                rows = slice(p * PAGE_SIZE, (p + 1) * PAGE_SIZE)
                k_ref[p, h] = k[rows, sl]
                v_ref[p, h] = v[rows, sl]
    else:
        q_ref[...] = q
        k_ref[...] = k
        v_ref[...] = v


def _qkv_rope(x, g, w_qkv, pos, tm, head_major):
    b, t, _ = x.shape
    cos_t, sin_t = _rope_tables(pos)
    fixed = lambda i, j: (0, 0)
    if head_major:
        npg = tm // PAGE_SIZE
        out_specs = [pl.BlockSpec((None, ATT_HEADS, tm, HEAD_DIM), lambda i, j: (i, 0, j, 0)),
                     pl.BlockSpec((None, npg, ATT_HEADS, PAGE_SIZE, HEAD_DIM), lambda i, j: (i, j, 0, 0, 0)),
                     pl.BlockSpec((None, npg, ATT_HEADS, PAGE_SIZE, HEAD_DIM), lambda i, j: (i, j, 0, 0, 0))]
        kv_shape = (b, t // PAGE_SIZE, ATT_HEADS, PAGE_SIZE, HEAD_DIM)
        out_shape = [jax.ShapeDtypeStruct((b, ATT_HEADS, t, HEAD_DIM), F32),
                     jax.ShapeDtypeStruct(kv_shape, F32), jax.ShapeDtypeStruct(kv_shape, F32)]
    else:
        out_specs = [pl.BlockSpec((None, tm, D_MODEL), lambda i, j: (i, j, 0))] * 3
        out_shape = [jax.ShapeDtypeStruct((b, t, D_MODEL), F32)] * 3
    return pl.pallas_call(
        functools.partial(_qkv_kernel, head_major=head_major),
        grid=(b, t // tm),
        in_specs=[pl.BlockSpec((None, tm, D_MODEL), lambda i, j: (i, j, 0)),
                  pl.BlockSpec((1, D_MODEL), fixed),
                  pl.BlockSpec(w_qkv.shape, fixed),
                  pl.BlockSpec((tm, LANES), lambda i, j: (j, 0)),
                  pl.BlockSpec((tm, LANES), lambda i, j: (j, 0))],
        out_specs=out_specs,
        out_shape=out_shape,
        scratch_shapes=[pltpu.VMEM(w_qkv.shape, BF16)],
        compiler_params=_cparams(2),
        name="qkv_rope",
    )(x, g.reshape(1, -1), w_qkv, cos_t, sin_t)


def _top_blocks(gate, valid, n_keep):
    nb = gate.shape[0]
    blk = lax.broadcasted_iota(jnp.int32, gate.shape, 0)
    rank = jnp.zeros(gate.shape, F32)
    for m in range(nb):
        g_m = gate[m:m + 1, :]
        beats = (g_m > gate) | ((g_m == gate) & (m < blk))
        rank = rank + jnp.where(beats & valid[m:m + 1, :], 1.0, 0.0)
    return jnp.where(valid & (rank < n_keep), 1.0, 0.0)


def _moba_prompt_kernel(q_ref, k_ref, v_ref, o_ref, *, t, heads_per_step):
    nb = t // MOBA_BLOCK
    neg = jnp.float32(-jnp.inf)
    causal = (lax.broadcasted_iota(jnp.int32, (MOBA_BLOCK, MOBA_BLOCK), 1)
              <= lax.broadcasted_iota(jnp.int32, (MOBA_BLOCK, MOBA_BLOCK), 0))
    heads = []
    for hh in range(heads_per_step):
        q = q_ref[hh]
        k = k_ref[:, hh].reshape(t, HEAD_DIM)
        v = v_ref[:, hh].reshape(t, HEAD_DIM)
        kmean = jnp.concatenate(
            [jnp.mean(k[n * MOBA_BLOCK:(n + 1) * MOBA_BLOCK, :], axis=0, keepdims=True) for n in range(nb)], axis=0)
        gate = lax.dot_general(kmean, q, (((1,), (1,)), ((), ())), precision=HIGHEST,
                               preferred_element_type=F32)
        own_of_q = lax.broadcasted_iota(jnp.int32, (nb, t), 1) // MOBA_BLOCK
        valid = lax.broadcasted_iota(jnp.int32, (nb, t), 0) < own_of_q
        sel_t = _top_blocks(gate, valid, MOBA_TOPK)
        pad_rows = 16 - nb
        sel_t = jnp.concatenate([sel_t, jnp.zeros((pad_rows, t), F32)], axis=0)
        eye = jnp.where(lax.broadcasted_iota(jnp.int32, (16, LANES), 0)
                        == lax.broadcasted_iota(jnp.int32, (16, LANES), 1), 1.0, 0.0)
        sel = _tdot(sel_t, eye)
        kb = k.astype(BF16)
        vb = jnp.concatenate([v, jnp.ones((t, LANES - HEAD_DIM), F32)], axis=1).astype(BF16)
        heads.append((q, kb, vb, sel))

    o_rows = []
    for own in range(nb):
        rows = slice(own * MOBA_BLOCK, (own + 1) * MOBA_BLOCK)
        o_heads = []
        for q, kb, vb, sel in heads:
            qb = q[rows, :].astype(BF16)
            n_keys = (own + 1) * MOBA_BLOCK
            s_all = _bdot_t(qb, kb[:n_keys, :])
            scores = []
            for n in range(own):
                s = s_all[:, n * MOBA_BLOCK:(n + 1) * MOBA_BLOCK]
                scores.append(jnp.where(sel[rows, n:n + 1] > 0.0, s, neg))
            scores.append(jnp.where(causal, s_all[:, own * MOBA_BLOCK:], neg))
            m_elem = scores[0]
            for s in scores[1:]:
                m_elem = jnp.maximum(m_elem, s)
            m = m_elem.max(axis=-1, keepdims=True)
            p_all = jnp.concatenate([jnp.exp(s - m).astype(BF16) for s in scores], axis=1)
            acc = jnp.dot(p_all, vb[:n_keys, :], preferred_element_type=F32)
            o_heads.append(acc[:, :HEAD_DIM] / acc[:, HEAD_DIM:HEAD_DIM + 1])
        o_rows.append(jnp.concatenate(o_heads, axis=1))
    o_ref[...] = jnp.concatenate(o_rows, axis=0)


def _moba_prompt(q, k_rows, v_rows):
    b, n_h, t, _ = q.shape
    hps = LANES // HEAD_DIM
    npg = t // PAGE_SIZE
    kv_spec = pl.BlockSpec((None, npg, hps, PAGE_SIZE, HEAD_DIM), lambda i, j: (i, 0, j, 0, 0))
    return pl.pallas_call(
        functools.partial(_moba_prompt_kernel, t=t, heads_per_step=hps),
        grid=(b, n_h // hps),
        in_specs=[pl.BlockSpec((None, hps, t, HEAD_DIM), lambda i, j: (i, j, 0, 0)), kv_spec, kv_spec],
        out_specs=pl.BlockSpec((None, t, LANES), lambda i, j: (i, 0, j)),
        out_shape=jax.ShapeDtypeStruct((b, t, n_h * HEAD_DIM), F32),
        compiler_params=_cparams(2),
        name="moba_prompt",
    )(q, k_rows, v_rows)


def _kmean_select_kernel(pt_ref, q_ref, ptv_ref, *refs, pages_per_step, n_q, n_blk):
    page_refs = refs[:pages_per_step]
    sel_ref = refs[pages_per_step]
    kmt_ref = refs[pages_per_step + 1]
    s = pl.program_id(1)
    ppb = MOBA_BLOCK // PAGE_SIZE
    rows = ATT_HEADS * HEAD_DIM

    @pl.when(s == 0)
    def _():
        kmt_ref[...] = jnp.zeros_like(kmt_ref)

    lane = lax.broadcasted_iota(jnp.int32, (rows, LANES), 1)
    kmt = kmt_ref[...]
    for blk in range(pages_per_step // ppb):
        tot = page_refs[blk * ppb][...]
        for i in range(1, ppb):
            tot = tot + page_refs[blk * ppb + i][...]
        mean = tot.reshape(rows, PAGE_SIZE).sum(axis=-1, keepdims=True) * (1.0 / MOBA_BLOCK)
        kmt = jnp.where(lane == s * (pages_per_step // ppb) + blk, mean, kmt)
    kmt_ref[...] = kmt

    @pl.when(s == pl.num_programs(1) - 1)
    def _():
        q = q_ref[...]
        n_rows = q.shape[0]
        row_head = lax.broadcasted_iota(jnp.int32, (n_rows, LANES), 0) // n_q
        gate = jnp.zeros((n_rows, LANES), F32)
        for h in range(ATT_HEADS):
            g_h = jnp.dot(q, kmt[h * HEAD_DIM:(h + 1) * HEAD_DIM, :], precision=HIGHEST,
                          preferred_element_type=F32)
            gate = jnp.where(row_head == h, g_h, gate)
        blk_lane = lax.broadcasted_iota(jnp.int32, gate.shape, 1)
        neg = jnp.float32(-jnp.inf)
        gate = jnp.where(blk_lane < n_blk, gate, neg)
        pt_row = ptv_ref[...].astype(F32)
        pages = jnp.zeros(gate.shape, F32)
        for r in range(MOBA_TOPK):
            m = jnp.max(gate, axis=-1, keepdims=True)
            idx = jnp.min(jnp.where(gate == m, blk_lane, LANES), axis=-1, keepdims=True)
            for i in range(ppb):
                page = jnp.sum(jnp.where(blk_lane == idx * ppb + i, pt_row, 0.0), axis=-1, keepdims=True)
                pages = jnp.where(blk_lane == r * ppb + i, page, pages)
            gate = jnp.where(blk_lane == idx, neg, gate)
        sel_ref[...] = pages.astype(jnp.int32)


def _sample_select(q, cache_kt, page_table, n_blk, pages_per_step=32):
    db, n_h, n_q, _ = q.shape
    n_pages = page_table.shape[1]
    assert n_pages == LANES and n_blk <= LANES
    ppb = MOBA_BLOCK // PAGE_SIZE
    steps = n_blk * ppb // pages_per_step

    def page_spec(i):
        return pl.BlockSpec((None, n_h, HEAD_DIM, PAGE_SIZE),
                            lambda b, s, pt: (pt[b * n_pages + s * pages_per_step + i], 0, 0, 0))

    grid_spec = pltpu.PrefetchScalarGridSpec(
        num_scalar_prefetch=1,
        grid=(db, steps),
        in_specs=[pl.BlockSpec((None, n_h * n_q, HEAD_DIM), lambda b, s, pt: (b, 0, 0)),
                  pl.BlockSpec((None, 1, n_pages), lambda b, s, pt: (b, 0, 0))]
                 + [page_spec(i) for i in range(pages_per_step)],
        out_specs=pl.BlockSpec((None, n_h * n_q, LANES), lambda b, s, pt: (b, 0, 0)),
        scratch_shapes=[pltpu.VMEM((n_h * HEAD_DIM, LANES), F32)],
    )
    return pl.pallas_call(
        functools.partial(_kmean_select_kernel, pages_per_step=pages_per_step, n_q=n_q, n_blk=n_blk),
        grid_spec=grid_spec,
        out_shape=jax.ShapeDtypeStruct((db, n_h * n_q, LANES), jnp.int32),
        compiler_params=_cparams(2),
        name="sample_select",
    )(page_table.reshape(-1), q.reshape(db, n_h * n_q, HEAD_DIM), page_table.reshape(db, 1, n_pages),
      *([cache_kt] * pages_per_step))


def _sample_attn_kernel(pg_ref, q_ref, kn_ref, vn_ref, *refs, n_q, tiles_per_q):
    n_tiles = n_q * tiles_per_q
    k_refs = refs[:n_tiles]
    v_refs = refs[n_tiles:2 * n_tiles]
    o_ref = refs[2 * n_tiles]
    neg = jnp.float32(-jnp.inf)
    q = q_ref[...]
    kn, vn = kn_ref[...], vn_ref[...]
    rows8 = 8
    q8 = jnp.concatenate([q] * (rows8 // n_q), axis=0)
    kt_cat = jnp.concatenate([r[...] for r in k_refs], axis=1)
    vt_cat = jnp.concatenate([r[...] for r in v_refs], axis=1)
    s = _bdot(q8, kt_cat)
    row = lax.broadcasted_iota(jnp.int32, s.shape, 0) % n_q
    col = lax.broadcasted_iota(jnp.int32, s.shape, 1)
    s = jnp.where((col // (tiles_per_q * PAGE_SIZE)) == row, s, neg)
    row_o = lax.broadcasted_iota(jnp.int32, (rows8, 1), 0) % n_q
    s_own = [jnp.where(row_o >= c, jnp.sum(q8 * kn[c:c + 1, :], axis=-1, keepdims=True), neg)
             for c in range(n_q)]
    m = s.max(axis=-1, keepdims=True)
    for s_c in s_own:
        m = jnp.maximum(m, s_c)
    p = jnp.exp(s - m)
    den = p.sum(axis=-1, keepdims=True)
    acc = _bdot_t(p, vt_cat)
    for c, s_c in enumerate(s_own):
        p_c = jnp.exp(s_c - m)
        den = den + p_c
        acc = acc + p_c * vn[c:c + 1, :]
    o_ref[...] = (acc / den)[:n_q, :]


def _sample_attn(q, k_new, v_new, cache_kt, cache_vt, pages):
    db, n_h, n_q, _ = q.shape
    tiles_per_q = MOBA_TOPK * (MOBA_BLOCK // PAGE_SIZE)
    n_tiles = n_q * tiles_per_q

    def tile_spec(i):
        return pl.BlockSpec((None, None, HEAD_DIM, PAGE_SIZE),
                            lambda b, h, pg: (pg[(b * n_h + h) * n_tiles + i], h, 0, 0))

    small = pl.BlockSpec((None, None, n_q, HEAD_DIM), lambda b, h, pg: (b, h, 0, 0))
    grid_spec = pltpu.PrefetchScalarGridSpec(
        num_scalar_prefetch=1,
        grid=(db, n_h),
        in_specs=[small, small, small] + [tile_spec(i) for i in range(n_tiles)] * 2,
        out_specs=small,
    )
    return pl.pallas_call(
        functools.partial(_sample_attn_kernel, n_q=n_q, tiles_per_q=tiles_per_q),
        grid_spec=grid_spec,
        out_shape=jax.ShapeDtypeStruct((db, n_h, n_q, HEAD_DIM), F32),
        compiler_params=_cparams(2),
        name="sample_attn",
    )(pages, q, k_new, v_new, *([cache_kt] * n_tiles), *([cache_vt] * n_tiles))


def _even_layer(x, conv_prev, gla_prev, p, tm, conv_tt, gla_c, ffn_tm, ffn_tf):
    b, t, d = x.shape
    xf = x.reshape(b * t, d)
    glu, q, k, v, gate, la = _inproj(xf, p["norm_mix"], p["w_in"], p["w_a2"], p["b_a"], tm)
    as_bt = lambda a: a.reshape(b, t, a.shape[-1])
    conv_out, conv_state = _conv_module(as_bt(glu), conv_prev, p["conv_w"], p["conv_b"], p["ln_g"], p["ln_b"], conv_tt)
    o, gla_state = _gla(as_bt(q), as_bt(k), as_bt(la), as_bt(v), as_bt(gate),
                        gla_prev.reshape(b, GLA_HEADS * GLA_DK, GLA_DV), p["gla_norm_g"], gla_c, 4)
    h = _outproj(xf, [conv_out.reshape(b * t, -1), o.reshape(b * t, -1)], p["w_out"], tm)
    h = _ffn(h, p["norm_ffn"], p["ffn_w_gate"], p["ffn_w_up"], p["ffn_w_down"], ffn_tm, ffn_tf)
    return h.reshape(b, t, d), conv_state, gla_state.reshape(b, GLA_HEADS, GLA_DK, GLA_DV)


def kernel(x_prompt, x_sample, state_conv, state_gla, cache_k, cache_v, page_table, norm_mix_even, w_in_even, conv_w, conv_b, conv_ln_g, conv_ln_b, gla_w_a2, gla_b_a, gla_norm_g, w_out_even, norm_ffn_even, ffn_w_gate, ffn_w_up, ffn_w_down, norm_mix_odd, w_qkv_odd, w_out_odd, norm_ffn_odd, router_w, moe_w_gate, moe_w_up, moe_w_down, final_norm_g):
    bp, tp, d = x_prompt.shape
    db, ts, _ = x_sample.shape
    even = dict(norm_mix=norm_mix_even[0], w_in=w_in_even[0], conv_w=conv_w[0], conv_b=conv_b[0],
                ln_g=conv_ln_g[0], ln_b=conv_ln_b[0], w_a2=gla_w_a2[0], b_a=gla_b_a[0],
                gla_norm_g=gla_norm_g[0], w_out=w_out_even[0], norm_ffn=norm_ffn_even[0],
                ffn_w_gate=ffn_w_gate, ffn_w_up=ffn_w_up, ffn_w_down=ffn_w_down)

    hp, conv_p, gla_p = _even_layer(
        x_prompt, jnp.zeros((bp, CONV_W - 1, D_CONV), F32), jnp.zeros((bp, GLA_HEADS, GLA_DK, GLA_DV), F32),
        even, tm=512, conv_tt=256, gla_c=GLA_CHUNK, ffn_tm=1024, ffn_tf=256)

    m_s = db * ts
    xs = x_sample.reshape(m_s, d)
    glu, q, k, v, gate, la = _inproj(xs, even["norm_mix"], even["w_in"], even["w_a2"], even["b_a"], m_s)
    as_bt = lambda a: a.reshape(db, ts, a.shape[-1])
    conv_out_s, conv_s = _conv_module(as_bt(glu), state_conv[0], even["conv_w"], even["conv_b"],
                                      even["ln_g"], even["ln_b"], ts)
    pad_t = 16
    pad8 = lambda a: jnp.pad(as_bt(a), ((0, 0), (0, pad_t - ts), (0, 0)))
    o_s, gla_s = _gla(pad8(q), pad8(k), pad8(la), pad8(v), pad8(gate),
                      state_gla[0].reshape(db, GLA_HEADS * GLA_DK, GLA_DV), even["gla_norm_g"], pad_t, 4)
    hs = _outproj(xs, [conv_out_s.reshape(m_s, -1), o_s[:, :ts].reshape(m_s, -1)], even["w_out"], m_s)
    hs = _ffn(hs, even["norm_ffn"], ffn_w_gate, ffn_w_up, ffn_w_down, m_s, 256)
    gla_s = gla_s.reshape(db, GLA_HEADS, GLA_DK, GLA_DV)

    qp, k_rows, v_rows = _qkv_rope(hp, norm_mix_odd[0], w_qkv_odd[0], jnp.arange(tp), 256, True)
    op = _moba_prompt(qp, k_rows, v_rows)
    hp2, xn_p, gate_p, dest_p = _route(hp.reshape(bp * tp, d), op.reshape(bp * tp, d), w_out_odd[0],
                                       norm_ffn_odd[0], router_w[0], 1024)

    n_blocks = page_table.shape[1] * PAGE_SIZE // MOBA_BLOCK
    past_len = page_table.shape[1] * PAGE_SIZE
    pos_s = past_len + (jnp.arange(m_s) % ts)
    qs, ks, vs = _qkv_rope(hs.reshape(1, m_s, d), norm_mix_odd[0], w_qkv_odd[0], pos_s, m_s, False)
    to_heads = lambda a: a.reshape(db, ts, ATT_HEADS, HEAD_DIM).transpose(0, 2, 1, 3)
    qs, ks, vs = to_heads(qs), to_heads(ks), to_heads(vs)
    cache_kt, cache_vt = jnp.swapaxes(cache_k[0], -1, -2), jnp.swapaxes(cache_v[0], -1, -2)
    tiles_per_q = MOBA_TOPK * (MOBA_BLOCK // PAGE_SIZE)
    pages = _sample_select(qs, cache_kt, page_table, n_blocks)[:, :, :tiles_per_q].reshape(-1)
    os_ = _sample_attn(qs, ks, vs, cache_kt, cache_vt, pages)
    os_ = os_.transpose(0, 2, 1, 3).reshape(m_s, d)
    hs2 = _outproj(hs, [os_], w_out_odd[0], m_s)
    y_sample, moe_wg, moe_wu, moe_wd = _ffn(hs2, norm_ffn_odd[0], moe_w_gate[0], moe_w_up[0], moe_w_down[0], m_s, 512,
                                            router_w=router_w[0], final_g=final_norm_g, emit_bf16=True)
    y_sample = y_sample.reshape(db, ts, d)

    y_prompt = _experts(hp2, xn_p, gate_p, dest_p, moe_wg, moe_wu, moe_wd, final_norm_g,
                        tm=1024, tf=1792, rb=128, mb=256).reshape(bp, tp, d)

    return (y_prompt, y_sample, conv_p[None], gla_p[None], k_rows[None], v_rows[None],
            conv_s[None], gla_s[None], ks[None], vs[None])
```
